```python
import math
import jax
import jax.numpy as jnp
from jax import lax
import numpy as np

D_MODEL = 1024
BATCH = 2
SEQ = 8192
DEPTH = 4
DEC_BATCH = 32
DEC_SEQ = 8
PAST_LEN = 8192
PAGE_SIZE = 128

H_A = 4
DK_A = 128
DV_A = 128
CONV_A = 4
CHUNK_A = 64
H_B = 4
DH_B = 128
Q_BLOCK = 128
SB_BIAS_INIT = -8.0
W_C = 512
CONV_C = 3
N_GROUPS = 4
EXPERTS_PER_GROUP = 4
N_EXPERTS = N_GROUPS * EXPERTS_PER_GROUP
TOP_K = 2
D_EXPERT = 256

W_QK_A = H_A * DK_A
W_V_A = H_A * DV_A
W_QKV_A = 2 * W_QK_A + W_V_A
W_B = H_B * DH_B
D_IN = W_QKV_A + W_V_A + 2 * H_A + 3 * W_B + 3 * W_C + 3 * D_MODEL

DEEPNORM_ALPHA = (2.0 * DEPTH) ** 0.25
DEEPNORM_BETA = (8.0 * DEPTH) ** -0.25
LN_EPS = 1e-5
RMS_EPS = 1e-6

kernel_name = "hybrid_deltanet_stickbreak_shortconv_hmoe_step"


def layer_norm(x, g, b):
    xf = x.astype(jnp.float32)
    mu = jnp.mean(xf, axis=-1, keepdims=True)
    var = jnp.mean(jnp.square(xf - mu), axis=-1, keepdims=True)
    return ((xf - mu) * lax.rsqrt(var + LN_EPS) * g.astype(jnp.float32) + b.astype(jnp.float32)).astype(x.dtype)


def causal_dwconv(u, buf, w):
    width = w.shape[0]
    seq = u.shape[1]
    up = jnp.concatenate([buf.astype(u.dtype), u], axis=1)
    y = up[:, 0:seq] * w[0]
    for i in range(1, width):
        y = y + up[:, i:i + seq] * w[i]
    return y, up[:, up.shape[1] - (width - 1):]


def l2norm(x):
    xf = x.astype(jnp.float32)
    return xf * lax.rsqrt(jnp.sum(jnp.square(xf), axis=-1, keepdims=True) + RMS_EPS)


def gated_rmsnorm(o, z, w):
    of = o.astype(jnp.float32)
    of = of * lax.rsqrt(jnp.mean(jnp.square(of), axis=-1, keepdims=True) + RMS_EPS)
    return (of * w.astype(jnp.float32) * jax.nn.silu(z.astype(jnp.float32))).astype(z.dtype)


def gated_delta_rule(q, k, v, g, beta, s0):
    f32 = jnp.float32
    q, k, v = [t.astype(f32).transpose(0, 2, 1, 3) for t in (q, k, v)]
    g, beta = [t.astype(f32).transpose(0, 2, 1) for t in (g, beta)]
    bsz, nh, seq, dk = q.shape
    dv = v.shape[-1]
    c = min(CHUNK_A, seq)
    pad = (-seq) % c
    if pad:
        q, k, v = [jnp.pad(t, ((0, 0), (0, 0), (0, pad), (0, 0))) for t in (q, k, v)]
        g, beta = [jnp.pad(t, ((0, 0), (0, 0), (0, pad))) for t in (g, beta)]
    n = (seq + pad) // c
    q = q.reshape(bsz, nh, n, c, dk)
    k = k.reshape(bsz, nh, n, c, dk)
    v = v.reshape(bsz, nh, n, c, dv)
    g = g.reshape(bsz, nh, n, c)
    beta = beta.reshape(bsz, nh, n, c)
    gc = jnp.cumsum(g, axis=-1)
    idx = jnp.arange(c)
    causal = idx[:, None] >= idx[None, :]
    strict = idx[:, None] > idx[None, :]
    decay = jnp.exp(jnp.where(causal, gc[..., :, None] - gc[..., None, :], -jnp.inf))
    kb = k * beta[..., None]
    lmat = jnp.where(strict, jnp.einsum('bhnid,bhnjd->bhnij', kb, k) * decay, 0.0)
    tmat = lmat + jnp.eye(c, dtype=f32)
    rhs = jnp.concatenate([v * beta[..., None], kb * jnp.exp(gc)[..., None]], axis=-1)
    sol = lax.linalg.triangular_solve(tmat, rhs, left_side=True, lower=True, unit_diagonal=True)
    u, w = sol[..., :dv], sol[..., dv:]
    qk = jnp.einsum('bhnid,bhnjd->bhnij', q, k) * decay
    q_dec = q * jnp.exp(gc)[..., None]
    k_dec = k * jnp.exp(gc[..., -1:] - gc)[..., None]
    g_last = jnp.exp(gc[..., -1])
    xs = tuple(jnp.moveaxis(t, 2, 0) for t in (u, w, qk, q_dec, k_dec, g_last))

    def step(s, inp):
        u_i, w_i, qk_i, qd_i, kd_i, gl_i = inp
        v_new = u_i - jnp.einsum('bhcd,bhde->bhce', w_i, s)
        o_i = jnp.einsum('bhcd,bhde->bhce', qd_i, s) + jnp.einsum('bhij,bhje->bhie', qk_i, v_new)
        s = s * gl_i[..., None, None] + jnp.einsum('bhcd,bhce->bhde', kd_i, v_new)
        return s, o_i

    s_fin, o = lax.scan(step, s0.astype(f32), xs)
    o = jnp.moveaxis(o, 0, 2).reshape(bsz, nh, n * c, dv)[:, :, :seq]
    return o.transpose(0, 2, 1, 3), s_fin


def stick_breaking_attention(q, k, v, bias, pos0):
    bsz, seq, nh, dh = q.shape
    n_keys = k.shape[1]
    qb = min(Q_BLOCK, seq)
    nb = -(-seq // qb)
    pad = nb * qb - seq
    qf = jnp.pad(q.astype(jnp.float32) * (dh ** -0.5), ((0, 0), (0, pad), (0, 0), (0, 0)))
    q_blocks = jnp.moveaxis(qf.reshape(bsz, nb, qb, nh, dh), 1, 0)
    q_pos = (pos0 + jnp.arange(nb * qb, dtype=jnp.int32)).reshape(nb, qb)
    k_pos = jnp.arange(n_keys, dtype=jnp.int32)
    kf = k.astype(jnp.float32)
    bf = bias.astype(jnp.float32)[None, :, None, None]

    def block(args):
        q_blk, qp = args
        z = jnp.einsum('bqhd,bkhd->bhqk', q_blk, kf) + bf
        valid = k_pos[None, :] < qp[:, None]
        sp = jnp.where(valid, jax.nn.softplus(z), 0.0)
        rest = lax.cumsum(sp, axis=3, reverse=True) - sp
        a = jnp.where(valid, jnp.exp(jax.nn.log_sigmoid(z) - rest), 0.0)
        return jnp.einsum('bhqk,bkhd->bqhd', a.astype(v.dtype), v)

    out = lax.map(block, (q_blocks, q_pos))
    return jnp.moveaxis(out, 0, 1).reshape(bsz, nb * qb, nh, dh)[:, :seq]


def hier_moe(x, w_rg, b_rg, w_re, b_re, w_gu, w_down):
    bsz, seq, d = x.shape
    t = x.reshape(bsz * seq, d)
    n_tok = t.shape[0]
    gl = (t @ w_rg).astype(jnp.float32) + b_rg.astype(jnp.float32)
    pg = jax.nn.softmax(gl, axis=-1)
    gsel = jnp.argmax(gl, axis=-1)
    pg_sel = jnp.take_along_axis(pg, gsel[:, None], axis=-1)
    el = ((t @ w_re).astype(jnp.float32) + b_re.astype(jnp.float32)).reshape(n_tok, N_GROUPS, EXPERTS_PER_GROUP)
    el_sel = jnp.take_along_axis(el, gsel[:, None, None], axis=1)[:, 0]
    top_v, top_i = lax.top_k(el_sel, TOP_K)
    w_e = jax.nn.softmax(top_v, axis=-1) * pg_sel
    eid = gsel[:, None] * EXPERTS_PER_GROUP + top_i
    gates = jnp.sum(jax.nn.one_hot(eid, N_EXPERTS, dtype=jnp.float32) * w_e[..., None], axis=1)
    h = jnp.einsum('td,edf->tef', t, w_gu)
    h_gate, h_up = jnp.split(h, 2, axis=-1)
    act = jax.nn.silu(h_gate) * h_up * gates[..., None].astype(x.dtype)
    y = jnp.einsum('tef,efd->td', act, w_down)
    return y.reshape(bsz, seq, d)


def split_combined(h):
    sizes = [W_QKV_A, W_V_A, H_A, H_A, W_B, W_B, W_B, W_C, W_C, W_C, D_MODEL, D_MODEL, D_MODEL]
    points = []
    acc = 0
    for s in sizes[:-1]:
        acc += s
        points.append(acc)
    return jnp.split(h, points, axis=-1)


def layer_forward(x, s0, dbuf0, sbuf0, k_past, v_past, w_in, conv_a, a_log, dt_bias, norm_a, sb_bias, conv_c,
                  w_pa, w_pb, w_pc, w_o, ln_g, ln_b, w_rg, b_rg, w_re, b_re, w_gu, w_down):
    bsz, seq, _ = x.shape
    h = x @ w_in
    (qkv_a, z_a, b_a, a_a, q_b, k_b, v_b, c_in, c_bg, c_cg, g_a, g_b, g_c) = split_combined(h)

    qkv_c, dbuf = causal_dwconv(qkv_a, dbuf0, conv_a)
    qkv_c = jax.nn.silu(qkv_c)
    q_a, k_a, v_a = jnp.split(qkv_c, [W_QK_A, 2 * W_QK_A], axis=-1)
    q_a = l2norm(q_a.reshape(bsz, seq, H_A, DK_A)) * (DK_A ** -0.5)
    k_a = l2norm(k_a.reshape(bsz, seq, H_A, DK_A))
    v_a = v_a.reshape(bsz, seq, H_A, DV_A)
    beta = jax.nn.sigmoid(b_a.astype(jnp.float32))
    g = -jnp.exp(a_log.astype(jnp.float32)) * jax.nn.softplus(a_a.astype(jnp.float32) + dt_bias.astype(jnp.float32))
    o_a, s_new = gated_delta_rule(q_a, k_a, v_a, g, beta, s0)
    o_a = gated_rmsnorm(o_a, z_a.reshape(bsz, seq, H_A, DV_A), norm_a).reshape(bsz, seq, W_V_A)

    q_b = q_b.reshape(bsz, seq, H_B, DH_B)
    k_b = k_b.reshape(bsz, seq, H_B, DH_B)
    v_b = v_b.reshape(bsz, seq, H_B, DH_B)
    k_all = jnp.concatenate([k_past.astype(x.dtype), k_b], axis=1)
    v_all = jnp.concatenate([v_past.astype(x.dtype), v_b], axis=1)
    o_b = stick_breaking_attention(q_b, k_all, v_all, sb_bias, k_past.shape[1]).reshape(bsz, seq, W_B)

    u_c, sbuf = causal_dwconv(c_cg * c_in, sbuf0, conv_c)
    o_c = c_bg * u_c

    merged = (jax.nn.sigmoid(g_a) * (o_a @ w_pa) + jax.nn.sigmoid(g_b) * (o_b @ w_pb)
              + jax.nn.sigmoid(g_c) * (o_c @ w_pc))
    x = layer_norm(DEEPNORM_ALPHA * x + merged @ w_o, ln_g[0], ln_b[0])
    x = layer_norm(DEEPNORM_ALPHA * x + hier_moe(x, w_rg, b_rg, w_re, b_re, w_gu, w_down), ln_g[1], ln_b[1])
    return x, s_new, dbuf, sbuf, k_b, v_b


def setup_inputs(seed: int = 0) -> dict:
    key = jax.random.key(seed)
    ks = jax.random.split(key, 27)
    f32 = jnp.float32
    n_pages = PAST_LEN // PAGE_SIZE
    n_pool = (DEC_BATCH * n_pages * 5) // 4

    def nrm(k, shape, scale):
        return jax.random.normal(k, shape, f32) * scale

    page_table = jax.random.permutation(ks[7], n_pool)[:DEC_BATCH * n_pages].reshape(DEC_BATCH, n_pages).astype(jnp.int32)
    a_log = jnp.log(jax.random.uniform(ks[10], (DEPTH, H_A), f32, 1.0, 16.0))
    dt = jnp.exp(jax.random.uniform(ks[11], (DEPTH, H_A), f32, math.log(1e-3), math.log(1e-1)))
    dt_bias = dt + jnp.log(-jnp.expm1(-dt))
    return {
        "x_prompt": nrm(ks[0], (BATCH, SEQ, D_MODEL), 1.0),
        "x_sample": nrm(ks[1], (DEC_BATCH, DEC_SEQ, D_MODEL), 1.0),
        "cache_k": nrm(ks[2], (DEPTH, n_pool, PAGE_SIZE, H_B, DH_B), 1.0),
        "cache_v": nrm(ks[3], (DEPTH, n_pool, PAGE_SIZE, H_B, DH_B), 1.0),
        "state_delta": nrm(ks[4], (DEPTH, DEC_BATCH, H_A, DK_A, DV_A), 0.1),
        "state_dconv": nrm(ks[5], (DEPTH, DEC_BATCH, CONV_A - 1, W_QKV_A), 1.0),
        "state_sconv": nrm(ks[6], (DEPTH, DEC_BATCH, CONV_C - 1, W_C), 1.0),
        "page_table": page_table,
        "w_in": nrm(ks[8], (DEPTH, D_MODEL, D_IN), D_MODEL ** -0.5),
        "conv_a": nrm(ks[9], (DEPTH, CONV_A, W_QKV_A), CONV_A ** -0.5),
        "a_log": a_log,
        "dt_bias": dt_bias,
        "norm_a": 1.0 + nrm(ks[12], (DEPTH, DV_A), 0.02),
        "sb_bias": SB_BIAS_INIT + nrm(ks[26], (DEPTH, H_B), 0.1),
        "conv_c": nrm(ks[13], (DEPTH, CONV_C, W_C), CONV_C ** -0.5),
        "w_pa": nrm(ks[14], (DEPTH, W_V_A, D_MODEL), DEEPNORM_BETA * W_V_A ** -0.5),
        "w_pb": nrm(ks[15], (DEPTH, W_B, D_MODEL), DEEPNORM_BETA * W_B ** -0.5),
        "w_pc": nrm(ks[16], (DEPTH, W_C, D_MODEL), DEEPNORM_BETA * W_C ** -0.5),
        "w_o": nrm(ks[17], (DEPTH, D_MODEL, D_MODEL), DEEPNORM_BETA * D_MODEL ** -0.5),
        "ln_g": 1.0 + nrm(ks[18], (DEPTH, 2, D_MODEL), 0.02),
        "ln_b": nrm(ks[19], (DEPTH, 2, D_MODEL), 0.02),
        "w_rg": nrm(ks[20], (DEPTH, D_MODEL, N_GROUPS), D_MODEL ** -0.5),
        "b_rg": nrm(ks[21], (DEPTH, N_GROUPS), 0.01),
        "w_re": nrm(ks[22], (DEPTH, D_MODEL, N_EXPERTS), D_MODEL ** -0.5),
        "b_re": nrm(ks[23], (DEPTH, N_EXPERTS), 0.01),
        "w_gu": nrm(ks[24], (DEPTH, N_EXPERTS, D_MODEL, 2 * D_EXPERT), D_MODEL ** -0.5),
        "w_down": nrm(ks[25], (DEPTH, N_EXPERTS, D_EXPERT, D_MODEL), DEEPNORM_BETA * D_EXPERT ** -0.5),
    }


def reference(x_prompt, x_sample, cache_k, cache_v, state_delta, state_dconv, state_sconv, page_table,
              w_in, conv_a, a_log, dt_bias, norm_a, sb_bias, conv_c, w_pa, w_pb, w_pc, w_o, ln_g, ln_b,
              w_rg, b_rg, w_re, b_re, w_gu, w_down):
    bp = x_prompt.shape[0]
    bs = x_sample.shape[0]
    dt = x_prompt.dtype
    xp = x_prompt
    xs = x_sample
    p_delta, p_dconv, p_sconv, p_k, p_v = [], [], [], [], []
    s_delta, s_dconv, s_sconv, s_k, s_v = [], [], [], [], []
    for l in range(DEPTH):
        weights = (w_in[l], conv_a[l], a_log[l], dt_bias[l], norm_a[l], sb_bias[l], conv_c[l], w_pa[l], w_pb[l],
                   w_pc[l], w_o[l], ln_g[l], ln_b[l], w_rg[l], b_rg[l], w_re[l], b_re[l], w_gu[l], w_down[l])
        xp, sd, dc, sc, kr, vr = layer_forward(
            xp,
            jnp.zeros((bp, H_A, DK_A, DV_A), jnp.float32),
            jnp.zeros((bp, CONV_A - 1, W_QKV_A), dt),
            jnp.zeros((bp, CONV_C - 1, W_C), dt),
            jnp.zeros((bp, 0, H_B, DH_B), dt),
            jnp.zeros((bp, 0, H_B, DH_B), dt),
            *weights)
        p_delta.append(sd)
        p_dconv.append(dc)
        p_sconv.append(sc)
        p_k.append(kr)
        p_v.append(vr)
        k_past = cache_k[l][page_table].reshape(bs, -1, H_B, DH_B)
        v_past = cache_v[l][page_table].reshape(bs, -1, H_B, DH_B)
        xs, sd, dc, sc, kr, vr = layer_forward(
            xs, state_delta[l], state_dconv[l], state_sconv[l], k_past, v_past, *weights)
        s_delta.append(sd)
        s_dconv.append(dc)
        s_sconv.append(sc)
        s_k.append(kr)
        s_v.append(vr)
    return (xp, xs,
            jnp.stack(p_delta), jnp.stack(p_dconv), jnp.stack(p_sconv), jnp.stack(p_k), jnp.stack(p_v),
            jnp.stack(s_delta), jnp.stack(s_dconv), jnp.stack(s_sconv), jnp.stack(s_k), jnp.stack(s_v))
```

```python
import functools

import jax
import jax.numpy as jnp
from jax import lax
from jax.experimental import pallas as pl
from jax.experimental.pallas import tpu as pltpu

F32 = jnp.float32
BF16 = jnp.bfloat16

D_MODEL = 1024
DEPTH = 4
H_A = 4
DK_A = 128
DV_A = 128
CONV_A = 4
CHUNK_A = 64
H_B = 4
DH_B = 128
W_C = 512
CONV_C = 3
N_GROUPS = 4
EXPERTS_PER_GROUP = 4
N_EXPERTS = N_GROUPS * EXPERTS_PER_GROUP
D_EXPERT = 256
PAGE_SIZE = 128

W_QK_A = H_A * DK_A
W_V_A = H_A * DV_A
W_QKV_A = 2 * W_QK_A + W_V_A
W_B = H_B * DH_B

DEEPNORM_ALPHA = (2.0 * DEPTH) ** 0.25
LN_EPS = 1e-5
RMS_EPS = 1e-6

G_OFF, G_W = 0, 3 * D_MODEL
QKVA_OFF, QKVA_W = 3072, W_QKV_A
C_OFF, C_W = 4608, 3 * W_C
QKVB_OFF, QKVB_W = 6144, 3 * W_B
Z_OFF, Z_W = 7680, W_V_A
H_COLS = 8192
LANES = 128
SUBLANES = 8
HALO = SUBLANES

VMEM_LIMIT = 56 * 1024 * 1024


def _cparams(sem):
    return pltpu.CompilerParams(dimension_semantics=sem, vmem_limit_bytes=VMEM_LIMIT)


def _dot(a, b):
    return jnp.dot(a, b, preferred_element_type=F32)


def _dot_nt(a, b):
    return lax.dot_general(a, b, (((1,), (1,)), ((), ())), preferred_element_type=F32)


def _dot_tn(a, b):
    return lax.dot_general(a, b, (((0,), (0,)), ((), ())), preferred_element_type=F32)


def _split2(x):
    hi = x.astype(BF16)
    lo = (x - hi.astype(F32)).astype(BF16)
    return hi, lo


def _split3(x):
    hi = x.astype(BF16)
    r = x - hi.astype(F32)
    mid = r.astype(BF16)
    lo = (r - mid.astype(F32)).astype(BF16)
    return hi, mid, lo


def _mm_sel(sel, x):
    hi, mid, lo = _split3(x)
    return _dot(sel, hi) + _dot(sel, mid) + _dot(sel, lo)


def _mm_hl(a, b):
    ah, al = _split2(a)
    bh, bl = _split2(b)
    return _dot(ah, bh) + _dot(ah, bl) + _dot(al, bh)


def _softplus(x):
    return jnp.maximum(x, 0.0) + jnp.log1p(jnp.exp(-jnp.abs(x)))


def _sigmoid(x):
    return 1.0 / (1.0 + jnp.exp(-x))


def _silu(x):
    return x * _sigmoid(x)


def _layer_norm(r, g, b):
    mu = jnp.mean(r, axis=-1, keepdims=True)
    d = r - mu
    var = jnp.mean(d * d, axis=-1, keepdims=True)
    return d * lax.rsqrt(var + LN_EPS) * g + b


def _inproj_kernel(x_ref, w_ref, o_ref):
    o_ref[...] = _dot(x_ref[...].astype(BF16), w_ref[...])


def in_proj(x2d, w):
    t_rows = x2d.shape[0]
    tm = min(1024, t_rows)
    tn = 1024
    return pl.pallas_call(
        _inproj_kernel,
        grid=(t_rows // tm, H_COLS // tn),
        in_specs=[pl.BlockSpec((tm, D_MODEL), lambda i, j: (i, 0)),
                  pl.BlockSpec((D_MODEL, tn), lambda i, j: (0, j))],
        out_specs=pl.BlockSpec((tm, tn), lambda i, j: (i, j)),
        out_shape=jax.ShapeDtypeStruct((t_rows, H_COLS), F32),
        compiler_params=_cparams(("parallel", "arbitrary")),
        name="in_proj",
    )(x2d, w)


def _delta_kernel(x_ref, qkv_ref, z_ref, wba_ref, conv_ref, prm_ref, s0_ref, db0_ref,
                  o_ref, s_out_ref, tail_ref,
                  s_scr, ext_scr, q_scr, k_scr, v_scr, bg_scr, o_scr, *, tc, cp):
    c = CHUNK_A
    t = pl.program_id(1)

    @pl.when(t == 0)
    def _():
        s_scr[...] = s0_ref[0]
        ext_scr[0:HALO, :] = db0_ref[0]

    u = qkv_ref[0]
    ext_scr[HALO:HALO + tc, :] = u
    w = conv_ref[...]
    y = (ext_scr[HALO - 3:HALO - 3 + tc, :] * w[0:1, :] + ext_scr[HALO - 2:HALO - 2 + tc, :] * w[1:2, :]
         + ext_scr[HALO - 1:HALO - 1 + tc, :] * w[2:3, :] + u * w[3:4, :])
    tail = ext_scr[tc:tc + HALO, :]
    ext_scr[0:HALO, :] = tail
    tail_ref[0] = tail
    y = _silu(y)

    ba = _dot(x_ref[0].astype(BF16), wba_ref[...])
    beta = _sigmoid(ba)
    g = -jnp.exp(prm_ref[0:1, :]) * _softplus(ba + prm_ref[1:2, :])
    lane = lax.broadcasted_iota(jnp.int32, (tc, LANES), 1)
    bg = jnp.where(lane < H_A, beta, g)

    if cp > tc:
        zpad = jnp.zeros((cp - tc, LANES), F32)
        q_scr[tc:cp, :] = jnp.zeros((cp - tc, W_QK_A), F32)
        k_scr[tc:cp, :] = jnp.zeros((cp - tc, W_QK_A), F32)
        v_scr[tc:cp, :] = jnp.zeros((cp - tc, W_V_A), F32)
        bg_scr[tc:cp, :] = zpad
    bg_scr[0:tc, :] = bg
    for h in range(H_A):
        qh = y[:, h * DK_A:(h + 1) * DK_A]
        kh = y[:, W_QK_A + h * DK_A:W_QK_A + (h + 1) * DK_A]
        qn = qh * lax.rsqrt(jnp.sum(qh * qh, axis=-1, keepdims=True) + RMS_EPS) * (DK_A ** -0.5)
        kn = kh * lax.rsqrt(jnp.sum(kh * kh, axis=-1, keepdims=True) + RMS_EPS)
        q_scr[0:tc, h * DK_A:(h + 1) * DK_A] = qn
        k_scr[0:tc, h * DK_A:(h + 1) * DK_A] = kn
    v_scr[0:tc, :] = y[:, 2 * W_QK_A:]

    ri = lax.broadcasted_iota(jnp.int32, (c, c), 0)
    ci = lax.broadcasted_iota(jnp.int32, (c, c), 1)
    tril = (ri >= ci).astype(BF16)
    triu_f = (ri <= ci).astype(F32)
    ones = jnp.ones((c, c), BF16)

    def chunk(ic, carry):
        r0 = pl.multiple_of(ic * c, c)
        bgc = bg_scr[pl.ds(r0, c), :]
        for h in range(H_A):
            qc = q_scr[pl.ds(r0, c), h * DK_A:(h + 1) * DK_A]
            kc = k_scr[pl.ds(r0, c), h * DK_A:(h + 1) * DK_A]
            vc = v_scr[pl.ds(r0, c), h * DV_A:(h + 1) * DV_A]
            beta_c = bgc[:, h:h + 1]
            g_c = bgc[:, H_A + h:H_A + h + 1]
            gb = jnp.broadcast_to(g_c, (c, LANES))
            gc = _mm_sel(tril, gb)
            gc_row = _mm_sel(ones, gb[:, :c] * triu_f)
            dec = jnp.exp(jnp.where(ri >= ci, gc[:, :c] - gc_row, -jnp.inf))
            kb = kc * beta_c
            kq = _dot_nt(jnp.concatenate([kb, qc], axis=0).astype(BF16), kc.astype(BF16))
            lmat = jnp.where(ri > ci, kq[:c] * dec, 0.0)
            qk = kq[c:] * dec
            egc = jnp.exp(gc)
            rhs = jnp.concatenate([vc * beta_c, kb * egc], axis=1)
            n = -lmat
            yy = n
            for _ in range(5):
                n = _mm_hl(n, n)
                yy = yy + n + _mm_hl(yy, n)
            sol = rhs + _mm_hl(yy, rhs)
            uu = sol[:, :DV_A]
            ww = sol[:, DV_A:]
            gl = gc[c - 1:c, :]
            q_dec = qc * egc
            k_dec = kc * jnp.exp(gl - gc)
            s = s_scr[h]
            ws = _dot(jnp.concatenate([ww, q_dec], axis=0).astype(BF16), s.astype(BF16))
            v_new = uu - ws[:c]
            o_c = ws[c:] + _dot(qk.astype(BF16), v_new.astype(BF16))
            s_scr[h] = s * jnp.exp(gl) + _dot_tn(k_dec.astype(BF16), v_new.astype(BF16))
            o_scr[pl.ds(r0, c), h * DV_A:(h + 1) * DV_A] = o_c
        return carry

    lax.fori_loop(0, cp // c, chunk, 0)
    s_out_ref[0] = s_scr[...]

    z = z_ref[0]
    nw = prm_ref[2:3, :]
    for h in range(H_A):
        oh = o_scr[0:tc, h * DV_A:(h + 1) * DV_A]
        zh = z[:, h * DV_A:(h + 1) * DV_A]
        oh = oh * lax.rsqrt(jnp.mean(oh * oh, axis=-1, keepdims=True) + RMS_EPS)
        o_ref[0, :, h * DV_A:(h + 1) * DV_A] = oh * nw * _silu(zh)


def delta_mixer(x, h, wba, conv_a, prm, s0, db0):
    b, l, _ = x.shape
    tc = min(256, l)
    cp = max(tc, CHUNK_A)
    nt = l // tc
    kern = functools.partial(_delta_kernel, tc=tc, cp=cp)
    return pl.pallas_call(
        kern,
        grid=(b, nt),
        in_specs=[pl.BlockSpec((1, tc, D_MODEL), lambda i, t: (i, t, 0)),
                  pl.BlockSpec((1, tc, QKVA_W), lambda i, t: (i, t, QKVA_OFF // QKVA_W)),
                  pl.BlockSpec((1, tc, Z_W), lambda i, t: (i, t, Z_OFF // Z_W)),
                  pl.BlockSpec((D_MODEL, LANES), lambda i, t: (0, 0)),
                  pl.BlockSpec((CONV_A, W_QKV_A), lambda i, t: (0, 0)),
                  pl.BlockSpec((SUBLANES, LANES), lambda i, t: (0, 0)),
                  pl.BlockSpec((1, H_A, DK_A, DV_A), lambda i, t: (i, 0, 0, 0)),
                  pl.BlockSpec((1, HALO, W_QKV_A), lambda i, t: (i, 0, 0))],
        out_specs=[pl.BlockSpec((1, tc, W_V_A), lambda i, t: (i, t, 0)),
                   pl.BlockSpec((1, H_A, DK_A, DV_A), lambda i, t: (i, 0, 0, 0)),
                   pl.BlockSpec((1, HALO, W_QKV_A), lambda i, t: (i, 0, 0))],
        out_shape=[jax.ShapeDtypeStruct((b, l, W_V_A), F32),
                   jax.ShapeDtypeStruct((b, H_A, DK_A, DV_A), F32),
                   jax.ShapeDtypeStruct((b, HALO, W_QKV_A), F32)],
        scratch_shapes=[pltpu.VMEM((H_A, DK_A, DV_A), F32),
                        pltpu.VMEM((tc + HALO, W_QKV_A), F32),
                        pltpu.VMEM((cp, W_QK_A), F32),
                        pltpu.VMEM((cp, W_QK_A), F32),
                        pltpu.VMEM((cp, W_V_A), F32),
                        pltpu.VMEM((cp, LANES), F32),
                        pltpu.VMEM((cp, W_V_A), F32)],
        compiler_params=_cparams(("parallel", "arbitrary")),
        name="delta_mixer",
    )(x, h, h, wba, conv_a, prm, s0, db0)


def _sb_tile(q, kb, vb, lm, bias, r_prev, valid):
    z = _dot_nt(q, kb) + bias
    sp = _softplus(z)
    if valid is not None:
        sp = jnp.where(valid, sp, 0.0)
    hi, lo = _split2(sp)
    csum = _dot(hi, lm) + _dot(lo, lm)
    a = jnp.exp(z - csum - r_prev)
    if valid is not None:
        a = jnp.where(valid, a, 0.0)
    return _dot(a.astype(BF16), vb), r_prev + csum[:, 0:1]


def _attn_prompt_kernel(bias_ref, q_ref, k_ref, v_ref, lm_ref, o_ref, kbf, vbf, acc, rsum, *, tq):
    h = pl.program_id(1)
    qi = pl.program_id(2)

    @pl.when(qi == 0)
    def _():
        kbf[...] = k_ref[0].astype(BF16)
        vbf[...] = v_ref[0].astype(BF16)

    q = (q_ref[0] * (DH_B ** -0.5)).astype(BF16)
    bias = bias_ref[h]
    lm = lm_ref[...]
    row = lax.broadcasted_iota(jnp.int32, (tq, tq), 0)
    col = lax.broadcasted_iota(jnp.int32, (tq, tq), 1)

    d0 = pl.multiple_of(qi * tq, tq)
    pv, r = _sb_tile(q, kbf[pl.ds(d0, tq), :], vbf[pl.ds(d0, tq), :], lm, bias,
                     jnp.zeros((tq, 1), F32), col < row)
    acc[...] = pv
    rsum[...] = r

    def body(i, carry):
        k0 = pl.multiple_of((qi - 1 - i) * tq, tq)
        pv, r = _sb_tile(q, kbf[pl.ds(k0, tq), :], vbf[pl.ds(k0, tq), :], lm, bias, rsum[...], None)
        acc[...] += pv
        rsum[...] = r
        return carry

    lax.fori_loop(0, qi, body, 0)
    o_ref[0] = acc[...]


def _lower_ones(n):
    r = lax.broadcasted_iota(jnp.int32, (n, n), 0)
    c = lax.broadcasted_iota(jnp.int32, (n, n), 1)
    return (r >= c).astype(BF16)


def attn_prompt(h, sb_bias):
    b, l, _ = h.shape
    tq = min(256, l)
    qb = QKVB_OFF // DH_B
    kern = functools.partial(_attn_prompt_kernel, tq=tq)
    grid_spec = pltpu.PrefetchScalarGridSpec(
        num_scalar_prefetch=0,
        grid=(b, H_B, l // tq),
        in_specs=[pl.BlockSpec(memory_space=pltpu.SMEM),
                  pl.BlockSpec((1, tq, DH_B), lambda i, hh, j: (i, j, qb + hh)),
                  pl.BlockSpec((1, l, DH_B), lambda i, hh, j: (i, 0, qb + H_B + hh)),
                  pl.BlockSpec((1, l, DH_B), lambda i, hh, j: (i, 0, qb + 2 * H_B + hh)),
                  pl.BlockSpec((tq, tq), lambda i, hh, j: (0, 0))],
        out_specs=pl.BlockSpec((1, tq, DH_B), lambda i, hh, j: (i, j, hh)),
        scratch_shapes=[pltpu.VMEM((l, DH_B), BF16),
                        pltpu.VMEM((l, DH_B), BF16),
                        pltpu.VMEM((tq, DH_B), F32),
                        pltpu.VMEM((tq, 1), F32)])
    return pl.pallas_call(
        kern,
        grid_spec=grid_spec,
        out_shape=jax.ShapeDtypeStruct((b, l, W_B), F32),
        compiler_params=_cparams(("parallel", "parallel", "arbitrary")),
        name="attn_prompt",
    )(sb_bias, h, h, h, _lower_ones(tq))


def _attn_sample_kernel(pt_ref, bias_ref, qkv_ref, *rest, seq, pages_per_step, n_steps):
    g_pages = pages_per_step
    k_refs = rest[:g_pages]
    v_refs = rest[g_pages:2 * g_pages]
    lm_ref = rest[2 * g_pages]
    o_ref = rest[2 * g_pages + 1]
    qbd, acc, rsum = rest[2 * g_pages + 2:]
    s = pl.program_id(1)
    rows = H_B * seq
    lm = lm_ref[...]
    rid = lax.broadcasted_iota(jnp.int32, (rows, 1), 0)
    bias = jnp.zeros((rows, 1), F32)
    for hh in range(H_B):
        bias = jnp.where((rid >= hh * seq) & (rid < (hh + 1) * seq), bias_ref[hh], bias)

    def visit(kp, vp, valid):
        pv, r = _sb_tile(qbd[...], kp.astype(BF16), vp.astype(BF16), lm, bias, rsum[...], valid)
        acc[...] += pv
        rsum[...] = r

    @pl.when(s == 0)
    def _():
        qkv = qkv_ref[0]
        q = qkv[:, 0:W_B] * (DH_B ** -0.5)
        lane = lax.broadcasted_iota(jnp.int32, (seq, W_B), 1)
        parts = [jnp.where((lane >= hh * DH_B) & (lane < (hh + 1) * DH_B), q, 0.0) for hh in range(H_B)]
        qbd[...] = jnp.concatenate(parts, axis=0).astype(BF16)
        acc[...] = jnp.zeros((rows, W_B), F32)
        rsum[...] = jnp.zeros((rows, 1), F32)
        zrows = jnp.zeros((PAGE_SIZE - seq, W_B), F32)
        k_own = jnp.concatenate([qkv[:, W_B:2 * W_B], zrows], axis=0)
        v_own = jnp.concatenate([qkv[:, 2 * W_B:3 * W_B], zrows], axis=0)
        key = lax.broadcasted_iota(jnp.int32, (rows, PAGE_SIZE), 1)
        qpos = lax.broadcasted_iota(jnp.int32, (rows, PAGE_SIZE), 0) % seq
        visit(k_own, v_own, key < qpos)

    def page(ref):
        return jnp.concatenate([ref[0, 0, :, hh, :] for hh in range(H_B)], axis=1)

    for gi in reversed(range(g_pages)):
        visit(page(k_refs[gi]), page(v_refs[gi]), None)

    @pl.when(s == n_steps - 1)
    def _():
        a = acc[...]
        o_ref[0] = jnp.concatenate(
            [a[hh * seq:(hh + 1) * seq, hh * DH_B:(hh + 1) * DH_B] for hh in range(H_B)], axis=1)


def attn_sample(h, cache_k, cache_v, page_table, sb_bias, layer):
    b, seq, _ = h.shape
    n_pages = page_table.shape[0] // b
    g_pages = min(4, n_pages)
    n_steps = n_pages // g_pages
    rows = H_B * seq

    def page_map(gi):
        def index_map(i, s, pt):
            return (layer, pt[i * n_pages + (n_steps - 1 - s) * g_pages + gi], 0, 0, 0)
        return index_map

    page_specs = [pl.BlockSpec((1, 1, PAGE_SIZE, H_B, DH_B), page_map(gi)) for gi in range(g_pages)]
    kern = functools.partial(_attn_sample_kernel, seq=seq, pages_per_step=g_pages, n_steps=n_steps)
    grid_spec = pltpu.PrefetchScalarGridSpec(
        num_scalar_prefetch=1,
        grid=(b, n_steps),
        in_specs=([pl.BlockSpec(memory_space=pltpu.SMEM),
                   pl.BlockSpec((1, seq, QKVB_W), lambda i, s, pt: (i, 0, QKVB_OFF // QKVB_W))]
                  + page_specs + page_specs
                  + [pl.BlockSpec((PAGE_SIZE, PAGE_SIZE), lambda i, s, pt: (0, 0))]),
        out_specs=pl.BlockSpec((1, seq, W_B), lambda i, s, pt: (i, 0, 0)),
        scratch_shapes=[pltpu.VMEM((rows, W_B), BF16),
                        pltpu.VMEM((rows, W_B), F32),
                        pltpu.VMEM((rows, 1), F32)])
    return pl.pallas_call(
        kern,
        grid_spec=grid_spec,
        out_shape=jax.ShapeDtypeStruct((b, seq, W_B), F32),
        compiler_params=_cparams(("parallel", "arbitrary")),
        name="attn_sample",
    )(page_table, sb_bias, h, *([cache_k] * g_pages), *([cache_v] * g_pages), _lower_ones(PAGE_SIZE))


def _merge_kernel(x_ref, g_ref, c_ref, oa_ref, ob_ref, wpa_ref, wpb_ref, wpc_ref, wo_ref, conv_ref, ln_ref, sb0_ref,
                  o_ref, tail_ref, ext_scr, *, tm):
    t = pl.program_id(1)

    @pl.when(t == 0)
    def _():
        ext_scr[0:HALO, :] = sb0_ref[0]

    cc = c_ref[0]
    u = cc[:, 2 * W_C:] * cc[:, :W_C]
    ext_scr[HALO:HALO + tm, :] = u
    w = conv_ref[...]
    y = ext_scr[HALO - 2:HALO - 2 + tm, :] * w[0:1, :] + ext_scr[HALO - 1:HALO - 1 + tm, :] * w[1:2, :] + u * w[2:3, :]
    tail = ext_scr[tm:tm + HALO, :]
    ext_scr[0:HALO, :] = tail
    tail_ref[0] = tail
    o_c = cc[:, W_C:2 * W_C] * y

    g = g_ref[0]
    merged = (_sigmoid(g[:, :D_MODEL]) * _dot(oa_ref[0].astype(BF16), wpa_ref[...])
              + _sigmoid(g[:, D_MODEL:2 * D_MODEL]) * _dot(ob_ref[0].astype(BF16), wpb_ref[...])
              + _sigmoid(g[:, 2 * D_MODEL:]) * _dot(o_c.astype(BF16), wpc_ref[...]))
    r = DEEPNORM_ALPHA * x_ref[0] + _dot(merged.astype(BF16), wo_ref[...])
    o_ref[0] = _layer_norm(r, ln_ref[0:1, :], ln_ref[1:2, :])


def merge(x, h, o_a, o_b, wpa, wpb, wpc, wo, conv_c, ln, sb0):
    b, l, _ = x.shape
    tm = min(256, l)
    kern = functools.partial(_merge_kernel, tm=tm)
    const = lambda i, t: (0, 0)
    return pl.pallas_call(
        kern,
        grid=(b, l // tm),
        in_specs=[pl.BlockSpec((1, tm, D_MODEL), lambda i, t: (i, t, 0)),
                  pl.BlockSpec((1, tm, G_W), lambda i, t: (i, t, G_OFF // G_W)),
                  pl.BlockSpec((1, tm, C_W), lambda i, t: (i, t, C_OFF // C_W)),
                  pl.BlockSpec((1, tm, W_V_A), lambda i, t: (i, t, 0)),
                  pl.BlockSpec((1, tm, W_B), lambda i, t: (i, t, 0)),
                  pl.BlockSpec((W_V_A, D_MODEL), const),
                  pl.BlockSpec((W_B, D_MODEL), const),
                  pl.BlockSpec((W_C, D_MODEL), const),
                  pl.BlockSpec((D_MODEL, D_MODEL), const),
                  pl.BlockSpec((CONV_C, W_C), const),
                  pl.BlockSpec((2, D_MODEL), const),
                  pl.BlockSpec((1, HALO, W_C), lambda i, t: (i, 0, 0))],
        out_specs=[pl.BlockSpec((1, tm, D_MODEL), lambda i, t: (i, t, 0)),
                   pl.BlockSpec((1, HALO, W_C), lambda i, t: (i, 0, 0))],
        out_shape=[jax.ShapeDtypeStruct((b, l, D_MODEL), F32),
                   jax.ShapeDtypeStruct((b, HALO, W_C), F32)],
        scratch_shapes=[pltpu.VMEM((tm + HALO, W_C), F32)],
        compiler_params=_cparams(("parallel", "arbitrary")),
        name="merge",
    )(x, h, h, o_a, o_b, wpa, wpb, wpc, wo, conv_c, ln, sb0)


def _moe_kernel(x_ref, wr_ref, br_ref, wgu_ref, wdn_ref, ln_ref, o_ref, xb, gates, acc, *, tm):
    e = pl.program_id(1)
    lane = lax.broadcasted_iota(jnp.int32, (tm, LANES), 1)

    @pl.when(e == 0)
    def _():
        x = x_ref[...]
        xh, xm, xl = _split3(x)
        wh, wm, wl = _split3(wr_ref[...])
        logits = (_dot(xh, wh) + _dot(xh, wm) + _dot(xm, wh) + _dot(xh, wl) + _dot(xl, wh) + _dot(xm, wm)
                  + br_ref[...])
        lanef = lane.astype(F32)
        big = float(LANES)
        is_g = lane < N_GROUPS
        gl = jnp.where(is_g, logits, -jnp.inf)
        gmax = jnp.max(gl, axis=-1, keepdims=True)
        gsel = jnp.min(jnp.where(gl == gmax, lanef, big), axis=-1, keepdims=True)
        pg_sel = 1.0 / jnp.sum(jnp.where(is_g, jnp.exp(gl - gmax), 0.0), axis=-1, keepdims=True)
        lo = N_GROUPS + gsel * EXPERTS_PER_GROUP
        ev = jnp.where((lanef >= lo) & (lanef < lo + EXPERTS_PER_GROUP), logits, -jnp.inf)
        v1 = jnp.max(ev, axis=-1, keepdims=True)
        i1 = jnp.min(jnp.where(ev == v1, lanef, big), axis=-1, keepdims=True)
        ev2 = jnp.where(lanef == i1, -jnp.inf, ev)
        v2 = jnp.max(ev2, axis=-1, keepdims=True)
        i2 = jnp.min(jnp.where(ev2 == v2, lanef, big), axis=-1, keepdims=True)
        e2 = jnp.exp(v2 - v1)
        den = 1.0 + e2
        gates[...] = jnp.where(lanef == i1, pg_sel / den, jnp.where(lanef == i2, pg_sel * e2 / den, 0.0))
        xb[...] = xh
        acc[...] = jnp.zeros((tm, D_MODEL), F32)

    ge = jnp.sum(jnp.where(lane == N_GROUPS + e, gates[...], 0.0), axis=-1, keepdims=True)
    hh = _dot(xb[...], wgu_ref[0])
    act = _silu(hh[:, :D_EXPERT]) * hh[:, D_EXPERT:] * ge
    acc[...] += _dot(act.astype(BF16), wdn_ref[0])

    @pl.when(e == N_EXPERTS - 1)
    def _():
        r = DEEPNORM_ALPHA * x_ref[...] + acc[...]
        o_ref[...] = _layer_norm(r, ln_ref[0:1, :], ln_ref[1:2, :])


def moe(x2d, wr, br, wgu, wdn, ln):
    t_rows = x2d.shape[0]
    tm = min(1024, t_rows)
    kern = functools.partial(_moe_kernel, tm=tm)
    return pl.pallas_call(
        kern,
        grid=(t_rows // tm, N_EXPERTS),
        in_specs=[pl.BlockSpec((tm, D_MODEL), lambda i, e: (i, 0)),
                  pl.BlockSpec((D_MODEL, LANES), lambda i, e: (0, 0)),
                  pl.BlockSpec((1, LANES), lambda i, e: (0, 0)),
                  pl.BlockSpec((1, D_MODEL, 2 * D_EXPERT), lambda i, e: (e, 0, 0)),
                  pl.BlockSpec((1, D_EXPERT, D_MODEL), lambda i, e: (e, 0, 0)),
                  pl.BlockSpec((2, D_MODEL), lambda i, e: (0, 0))],
        out_specs=pl.BlockSpec((tm, D_MODEL), lambda i, e: (i, 0)),
        out_shape=jax.ShapeDtypeStruct((t_rows, D_MODEL), F32),
        scratch_shapes=[pltpu.VMEM((tm, D_MODEL), BF16),
                        pltpu.VMEM((tm, LANES), F32),
                        pltpu.VMEM((tm, D_MODEL), F32)],
        compiler_params=_cparams(("parallel", "arbitrary")),
        name="moe",
    )(x2d, wr, br, wgu, wdn, ln)


def _prep_layer(l, w_in, conv_a, a_log, dt_bias, norm_a, conv_c, w_pa, w_pb, w_pc, w_o, ln_g, ln_b,
                w_rg, b_rg, w_re, b_re, w_gu, w_down):
    wi = w_in[l]
    o_z = W_QKV_A
    o_ba = o_z + W_V_A
    o_qb = o_ba + 2 * H_A
    o_c = o_qb + 3 * W_B
    o_g = o_c + 3 * W_C
    w_main = jnp.concatenate([wi[:, o_g:], wi[:, :W_QKV_A], wi[:, o_c:o_g], wi[:, o_qb:o_c], wi[:, o_z:o_ba]],
                             axis=1).astype(BF16)
    w_ba = jnp.pad(wi[:, o_ba:o_qb], ((0, 0), (0, LANES - 2 * H_A))).astype(BF16)
    prm = jnp.zeros((SUBLANES, LANES), F32)
    prm = prm.at[0, H_A:2 * H_A].set(a_log[l]).at[1, H_A:2 * H_A].set(dt_bias[l]).at[2, :DV_A].set(norm_a[l])
    w_r = jnp.pad(jnp.concatenate([w_rg[l], w_re[l]], axis=1), ((0, 0), (0, LANES - N_GROUPS - N_EXPERTS)))
    b_r = jnp.pad(jnp.concatenate([b_rg[l], b_re[l]]), (0, LANES - N_GROUPS - N_EXPERTS)).reshape(1, LANES)
    return dict(
        w_main=w_main, w_ba=w_ba, conv_a=conv_a[l], prm=prm, conv_c=conv_c[l],
        w_pa=w_pa[l].astype(BF16), w_pb=w_pb[l].astype(BF16), w_pc=w_pc[l].astype(BF16), w_o=w_o[l].astype(BF16),
        ln0=jnp.stack([ln_g[l, 0], ln_b[l, 0]]), ln1=jnp.stack([ln_g[l, 1], ln_b[l, 1]]),
        w_r=w_r, b_r=b_r, w_gu=w_gu[l].astype(BF16), w_down=w_down[l].astype(BF16))


def _pad_tail(buf):
    return jnp.pad(buf, ((0, 0), (HALO - buf.shape[1], 0), (0, 0)))


def _layer(x, p, s0, db0, sb0, attn_fn):
    b, l, _ = x.shape
    h = in_proj(x.reshape(b * l, D_MODEL), p["w_main"]).reshape(b, l, H_COLS)
    o_a, s_new, dtail = delta_mixer(x, h, p["w_ba"], p["conv_a"], p["prm"], s0, _pad_tail(db0))
    o_b = attn_fn(h)
    x1, stail = merge(x, h, o_a, o_b, p["w_pa"], p["w_pb"], p["w_pc"], p["w_o"], p["conv_c"], p["ln0"], _pad_tail(sb0))
    x2 = moe(x1.reshape(b * l, D_MODEL), p["w_r"], p["b_r"], p["w_gu"], p["w_down"], p["ln1"]).reshape(b, l, D_MODEL)
    k_rows = h[:, :, QKVB_OFF + W_B:QKVB_OFF + 2 * W_B].reshape(b, l, H_B, DH_B)
    v_rows = h[:, :, QKVB_OFF + 2 * W_B:QKVB_OFF + 3 * W_B].reshape(b, l, H_B, DH_B)
    return x2, s_new, dtail[:, HALO - (CONV_A - 1):], stail[:, HALO - (CONV_C - 1):], k_rows, v_rows


def kernel(x_prompt, x_sample, cache_k, cache_v, state_delta, state_dconv, state_sconv, page_table, w_in, conv_a, a_log, dt_bias, norm_a, sb_bias, conv_c, w_pa, w_pb, w_pc, w_o, ln_g, ln_b, w_rg, b_rg, w_re, b_re, w_gu, w_down):
    bp = x_prompt.shape[0]
    depth = w_in.shape[0]
    pt =page_table.reshape(-1).astype(jnp.int32)
    xp, xs = x_prompt, x_sample
    outs_p = [[] for _ in range(5)]
    outs_s = [[] for _ in range(5)]
    for l in range(depth):
        p = _prep_layer(l, w_in, conv_a, a_log, dt_bias, norm_a, conv_c, w_pa, w_pb, w_pc, w_o, ln_g, ln_b,
                        w_rg, b_rg, w_re, b_re, w_gu, w_down)
        bias = sb_bias[l]
        res = _layer(xp, p,
                     jnp.zeros((bp, H_A, DK_A, DV_A), F32),
                     jnp.zeros((bp, CONV_A - 1, W_QKV_A), F32),
                     jnp.zeros((bp, CONV_C - 1, W_C), F32),
                     lambda h: attn_prompt(h, bias))
        xp = res[0]
        for acc_list, r in zip(outs_p, res[1:]):
            acc_list.append(r)
        res = _layer(xs, p, state_delta[l], state_dconv[l], state_sconv[l],
                     lambda h: attn_sample(h, cache_k, cache_v, pt, bias, l))
        xs = res[0]
        for acc_list, r in zip(outs_s, res[1:]):
            acc_list.append(r)
    return (xp, xs, *[jnp.stack(o) for o in outs_p], *[jnp.stack(o) for o in outs_s])
```

```python
import functools

import jax
import jax.numpy as jnp
from jax import lax
from jax.experimental import pallas as pl
from jax.experimental.pallas import tpu as pltpu

F32 = jnp.float32
BF16 = jnp.bfloat16

D_MODEL = 1024
DEPTH = 4
H_A = 4
DK_A = 128
DV_A = 128
CONV_A = 4
CHUNK_A = 64
H_B = 4
DH_B = 128
W_C = 512
CONV_C = 3
N_GROUPS = 4
EXPERTS_PER_GROUP = 4
N_EXPERTS = N_GROUPS * EXPERTS_PER_GROUP
D_EXPERT = 256
PAGE_SIZE = 128

W_QK_A = H_A * DK_A
W_V_A = H_A * DV_A
W_QKV_A = 2 * W_QK_A + W_V_A
W_B = H_B * DH_B

DEEPNORM_ALPHA = (2.0 * DEPTH) ** 0.25
LN_EPS = 1e-5
RMS_EPS = 1e-6

G_OFF, G_W = 0, 3 * D_MODEL
QKVA_OFF, QKVA_W = 3072, W_QKV_A
C_OFF, C_W = 4608, 3 * W_C
QKVB_OFF, QKVB_W = 6144, 3 * W_B
Z_OFF, Z_W = 7680, W_V_A
H_COLS = 8192
LANES = 128
SUBLANES = 8
HALO = SUBLANES

VMEM_LIMIT = 56 * 1024 * 1024


def _cparams(sem):
    return pltpu.CompilerParams(dimension_semantics=sem, vmem_limit_bytes=VMEM_LIMIT)


def _dot(a, b):
    return jnp.dot(a, b, preferred_element_type=F32)


def _dot_nt(a, b):
    return lax.dot_general(a, b, (((1,), (1,)), ((), ())), preferred_element_type=F32)


def _dot_tn(a, b):
    return lax.dot_general(a, b, (((0,), (0,)), ((), ())), preferred_element_type=F32)


def _split2(x):
    hi = x.astype(BF16)
    lo = (x - hi.astype(F32)).astype(BF16)
    return hi, lo


def _split3(x):
    hi = x.astype(BF16)
    r = x - hi.astype(F32)
    mid = r.astype(BF16)
    lo = (r - mid.astype(F32)).astype(BF16)
    return hi, mid, lo


def _mm_sel(sel, x):
    hi, mid, lo = _split3(x)
    return _dot(sel, hi) + _dot(sel, mid) + _dot(sel, lo)


def _mm_hl(a, b):
    ah, al = _split2(a)
    bh, bl = _split2(b)
    return _dot(ah, bh) + _dot(ah, bl) + _dot(al, bh)


def _softplus(x):
    return jnp.maximum(x, 0.0) + jnp.log1p(jnp.exp(-jnp.abs(x)))


def _sigmoid(x):
    return 1.0 / (1.0 + jnp.exp(-x))


def _silu(x):
    return x * _sigmoid(x)


def _layer_norm(r, g, b):
    mu = jnp.mean(r, axis=-1, keepdims=True)
    d = r - mu
    var = jnp.mean(d * d, axis=-1, keepdims=True)
    return d * lax.rsqrt(var + LN_EPS) * g + b


def _inproj_kernel(x_ref, w_ref, o_ref):
    o_ref[...] = _dot(x_ref[...].astype(BF16), w_ref[...])


def in_proj(x2d, w):
    t_rows = x2d.shape[0]
    tm = min(1024, t_rows)
    tn = 1024
    return pl.pallas_call(
        _inproj_kernel,
        grid=(t_rows // tm, H_COLS // tn),
        in_specs=[pl.BlockSpec((tm, D_MODEL), lambda i, j: (i, 0)),
                  pl.BlockSpec((D_MODEL, tn), lambda i, j: (0, j))],
        out_specs=pl.BlockSpec((tm, tn), lambda i, j: (i, j)),
        out_shape=jax.ShapeDtypeStruct((t_rows, H_COLS), F32),
        compiler_params=_cparams(("parallel", "arbitrary")),
        name="in_proj",
    )(x2d, w)


def _delta_kernel(x_ref, qkv_ref, z_ref, wba_ref, conv_ref, prm_ref, s0_ref, db0_ref,
                  o_ref, s_out_ref, tail_ref,
                  s_scr, ext_scr, q_scr, k_scr, v_scr, bg_scr, o_scr, u_scr, wq_scr, qk_scr, kd_scr, gl_scr, *, tc, cp):
    c = CHUNK_A
    t = pl.program_id(1)

    @pl.when(t == 0)
    def _():
        s_scr[...] = s0_ref[0]
        ext_scr[0:HALO, :] = db0_ref[0]

    u = qkv_ref[0]
    ext_scr[HALO:HALO + tc, :] = u
    w = conv_ref[...]
    y = (ext_scr[HALO - 3:HALO - 3 + tc, :] * w[0:1, :] + ext_scr[HALO - 2:HALO - 2 + tc, :] * w[1:2, :]
         + ext_scr[HALO - 1:HALO - 1 + tc, :] * w[2:3, :] + u * w[3:4, :])
    tail = ext_scr[tc:tc + HALO, :]
    ext_scr[0:HALO, :] = tail
    tail_ref[0] = tail
    y = _silu(y)

    ba = _dot(x_ref[0].astype(BF16), wba_ref[...])
    beta = _sigmoid(ba)
    g = -jnp.exp(prm_ref[0:1, :]) * _softplus(ba + prm_ref[1:2, :])
    lane = lax.broadcasted_iota(jnp.int32, (tc, LANES), 1)
    bg = jnp.where(lane < H_A, beta, g)

    if cp > tc:
        zpad = jnp.zeros((cp - tc, LANES), F32)
        q_scr[tc:cp, :] = jnp.zeros((cp - tc, W_QK_A), F32)
        k_scr[tc:cp, :] = jnp.zeros((cp - tc, W_QK_A), F32)
        v_scr[tc:cp, :] = jnp.zeros((cp - tc, W_V_A), F32)
        bg_scr[tc:cp, :] = zpad
    bg_scr[0:tc, :] = bg
    for h in range(H_A):
        qh = y[:, h * DK_A:(h + 1) * DK_A]
        kh = y[:, W_QK_A + h * DK_A:W_QK_A + (h + 1) * DK_A]
        qn = qh * lax.rsqrt(jnp.sum(qh * qh, axis=-1, keepdims=True) + RMS_EPS) * (DK_A ** -0.5)
        kn = kh * lax.rsqrt(jnp.sum(kh * kh, axis=-1, keepdims=True) + RMS_EPS)
        q_scr[0:tc, h * DK_A:(h + 1) * DK_A] = qn
        k_scr[0:tc, h * DK_A:(h + 1) * DK_A] = kn
    v_scr[0:tc, :] = y[:, 2 * W_QK_A:]

    ri = lax.broadcasted_iota(jnp.int32, (c, c), 0)
    ci = lax.broadcasted_iota(jnp.int32, (c, c), 1)
    tril = (ri >= ci).astype(BF16)
    triu_f = (ri <= ci).astype(F32)
    ones = jnp.ones((c, c), BF16)

    def prep_many(ins):
        qcs, kcs, vcs, betas, gs = zip(*ins)
        gbs = [jnp.broadcast_to(g_c, (c, LANES)) for g_c in gs]
        gcs = [_mm_sel(tril, gb) for gb in gbs]
        grows = [_mm_sel(ones, gb[:, :c] * triu_f) for gb in gbs]
        decs = [jnp.exp(jnp.where(ri >= ci, gc[:, :c] - gr, -jnp.inf)) for gc, gr in zip(gcs, grows)]
        kbs = [kc * b for kc, b in zip(kcs, betas)]
        kqs = [_dot_nt(jnp.concatenate([kb, qc], axis=0).astype(BF16), kc.astype(BF16))
               for kb, qc, kc in zip(kbs, qcs, kcs)]
        qks = [kq[c:] * dec for kq, dec in zip(kqs, decs)]
        egcs = [jnp.exp(gc) for gc in gcs]
        rhss = [jnp.concatenate([vc * b, kb * egc], axis=1) for vc, b, kb, egc in zip(vcs, betas, kbs, egcs)]
        ns = [-jnp.where(ri > ci, kq[:c] * dec, 0.0) for kq, dec in zip(kqs, decs)]
        yys = ns
        for _ in range(5):
            nbs = [n.astype(BF16) for n in ns]
            ns = [_dot(nb, nb) for nb in nbs]
            prods = [_dot(yy.astype(BF16), n.astype(BF16)) for yy, n in zip(yys, ns)]
            yys = [yy + n + p for yy, n, p in zip(yys, ns, prods)]
        sols = [rhs + _dot(yy.astype(BF16), rhs.astype(BF16)) for rhs, yy in zip(rhss, yys)]
        gls = [gc[c - 1:c, :] for gc in gcs]
        outs = []
        for sol, qc, kc, qk, egc, gc, gl in zip(sols, qcs, kcs, qks, egcs, gcs, gls):
            wq = jnp.concatenate([sol[:, DV_A:], qc * egc], axis=0).astype(BF16)
            k_dec = (kc * jnp.exp(gl - gc)).astype(BF16)
            outs.append((sol[:, :DV_A], wq, qk.astype(BF16), k_dec,
                         jnp.broadcast_to(jnp.exp(gl), (SUBLANES, LANES))))
        return outs

    def prep(it, carry):
        where, ins = [], []
        for j in range(prep_unroll):
            ic = it * prep_unroll + j
            r0 = pl.multiple_of(ic * c, c)
            bgc = bg_scr[pl.ds(r0, c), :]
            for h in range(H_A):
                where.append((ic, r0, h))
                ins.append((q_scr[pl.ds(r0, c), h * DK_A:(h + 1) * DK_A], k_scr[pl.ds(r0, c), h * DK_A:(h + 1) * DK_A],
                            v_scr[pl.ds(r0, c), h * DV_A:(h + 1) * DV_A], bgc[:, h:h + 1], bgc[:, H_A + h:H_A + h + 1]))
        outs = prep_many(ins)
        for (ic, r0, h), (uu, wq, qk, k_dec, egl) in zip(where, outs):
            r2 = pl.multiple_of(ic * 2 * c, 2 * c)
            r8 = pl.multiple_of(ic * SUBLANES, SUBLANES)
            u_scr[pl.ds(r0, c), h * DV_A:(h + 1) * DV_A] = uu
            wq_scr[pl.ds(r2, 2 * c), h * DK_A:(h + 1) * DK_A] = wq
            qk_scr[pl.ds(r0, c), h * LANES:h * LANES + c] = qk
            kd_scr[pl.ds(r0, c), h * DK_A:(h + 1) * DK_A] = k_dec
            gl_scr[pl.ds(r8, SUBLANES), h * LANES:(h + 1) * LANES] = egl
        return carry

    def scan(ic, carry):
        r0 = pl.multiple_of(ic * c, c)
        r2 = pl.multiple_of(ic * 2 * c, 2 * c)
        r8 = pl.multiple_of(ic * SUBLANES, SUBLANES)
        hs = range(H_A)
        us = [u_scr[pl.ds(r0, c), h * DV_A:(h + 1) * DV_A] for h in hs]
        wqs = [wq_scr[pl.ds(r2, 2 * c), h * DK_A:(h + 1) * DK_A] for h in hs]
        qks = [qk_scr[pl.ds(r0, c), h * LANES:h * LANES + c] for h in hs]
        kds = [kd_scr[pl.ds(r0, c), h * DK_A:(h + 1) * DK_A] for h in hs]
        egls = [gl_scr[pl.ds(r8, 1), h * LANES:(h + 1) * LANES] for h in hs]
        ss = [s_scr[h] for h in hs]
        wss = [_dot(wq, s.astype(BF16)) for wq, s in zip(wqs, ss)]
        vns = [(uu - ws[:c]).astype(BF16) for uu, ws in zip(us, wss)]
        ocs = [ws[c:] + _dot(qk, vn) for ws, qk, vn in zip(wss, qks, vns)]
        sns = [s * egl + _dot_tn(kd, vn) for s, egl, kd, vn in zip(ss, egls, kds, vns)]
        for h in hs:
            o_scr[pl.ds(r0, c), h * DV_A:(h + 1) * DV_A] = ocs[h]
            s_scr[h] = sns[h]
        return carry

    n_chunks = cp // c
    prep_unroll = 2 if n_chunks % 2 == 0 else 1
    lax.fori_loop(0, n_chunks // prep_unroll, prep, 0)
    lax.fori_loop(0, n_chunks, scan, 0)
    s_out_ref[0] = s_scr[...]

    z = z_ref[0]
    nw = prm_ref[2:3, :]
    for h in range(H_A):
        oh = o_scr[0:tc, h * DV_A:(h + 1) * DV_A]
        zh = z[:, h * DV_A:(h + 1) * DV_A]
        oh = oh * lax.rsqrt(jnp.mean(oh * oh, axis=-1, keepdims=True) + RMS_EPS)
        o_ref[0, :, h * DV_A:(h + 1) * DV_A] = oh * nw * _silu(zh)


def delta_mixer(x, h, wba, conv_a, prm, s0, db0):
    b, l, _ = x.shape
    tc = min(256, l)
    cp = max(tc, CHUNK_A)
    nt = l // tc
    kern = functools.partial(_delta_kernel, tc=tc, cp=cp)
    return pl.pallas_call(
        kern,
        grid=(b, nt),
        in_specs=[pl.BlockSpec((1, tc, D_MODEL), lambda i, t: (i, t, 0)),
                  pl.BlockSpec((1, tc, QKVA_W), lambda i, t: (i, t, QKVA_OFF // QKVA_W)),
                  pl.BlockSpec((1, tc, Z_W), lambda i, t: (i, t, Z_OFF // Z_W)),
                  pl.BlockSpec((D_MODEL, LANES), lambda i, t: (0, 0)),
                  pl.BlockSpec((CONV_A, W_QKV_A), lambda i, t: (0, 0)),
                  pl.BlockSpec((SUBLANES, LANES), lambda i, t: (0, 0)),
                  pl.BlockSpec((1, H_A, DK_A, DV_A), lambda i, t: (i, 0, 0, 0)),
                  pl.BlockSpec((1, HALO, W_QKV_A), lambda i, t: (i, 0, 0))],
        out_specs=[pl.BlockSpec((1, tc, W_V_A), lambda i, t: (i, t, 0)),
                   pl.BlockSpec((1, H_A, DK_A, DV_A), lambda i, t: (i, 0, 0, 0)),
                   pl.BlockSpec((1, HALO, W_QKV_A), lambda i, t: (i, 0, 0))],
        out_shape=[jax.ShapeDtypeStruct((b, l, W_V_A), F32),
                   jax.ShapeDtypeStruct((b, H_A, DK_A, DV_A), F32),
                   jax.ShapeDtypeStruct((b, HALO, W_QKV_A), F32)],
        scratch_shapes=[pltpu.VMEM((H_A, DK_A, DV_A), F32),
                        pltpu.VMEM((tc + HALO, W_QKV_A), F32),
                        pltpu.VMEM((cp, W_QK_A), F32),
                        pltpu.VMEM((cp, W_QK_A), F32),
                        pltpu.VMEM((cp, W_V_A), F32),
                        pltpu.VMEM((cp, LANES), F32),
                        pltpu.VMEM((cp, W_V_A), F32),
                        pltpu.VMEM((cp, W_V_A), F32),
                        pltpu.VMEM((2 * cp, W_QK_A), BF16),
                        pltpu.VMEM((cp, H_A * LANES), BF16),
                        pltpu.VMEM((cp, W_QK_A), BF16),
                        pltpu.VMEM((cp // CHUNK_A * SUBLANES, H_A * LANES), F32)],
        compiler_params=_cparams(("parallel", "arbitrary")),
        name="delta_mixer",
    )(x, h, h, wba, conv_a, prm, s0, db0)


def _sb_group(tiles, lm, bias, r0, chained, first_key_cols=1):
    zs, cs, tots = _sb_scores([(t[0], t[1], t[3]) for t in tiles], lm, bias, first_key_cols)
    return _sb_apply(zs, cs, tots, [t[2] for t in tiles], [t[3] for t in tiles], r0, chained)


def _sb_scores(tiles, lm, bias, first_key_cols):
    zs = [_dot_nt(q, kb) + bias for q, kb, _ in tiles]
    cs, tots = _sb_cumulate(zs, [t[2] for t in tiles], lm, first_key_cols)
    return zs, cs, tots


def _sb_cumulate(zs, valids, lm, first_key_cols):
    sps = [jnp.maximum(z, 0.0) + jnp.log(1.0 + jnp.exp(-jnp.abs(z))) for z in zs]
    sps = [sp if v is None else jnp.where(v, sp, 0.0) for sp, v in zip(sps, valids)]
    later = [_dot(sp.astype(BF16), lm) for sp in sps]
    tots = [lt[:, 0:1] + jnp.sum(sp[:, 0:first_key_cols], axis=-1, keepdims=True) for lt, sp in zip(later, sps)]
    return [sp + lt for sp, lt in zip(sps, later)], tots


def _sb_apply(zs, cs, tots, vbs, valids, r0, chained):
    rs, ws = [], []
    r = r0
    for i, (z, c, tot, valid) in enumerate(zip(zs, cs, tots, valids)):
        r_prev = r if chained else r0[i]
        a = jnp.exp(z - c - r_prev)
        if valid is not None:
            a = jnp.where(valid, a, 0.0)
        ws.append(a.astype(BF16))
        r = r_prev + tot
        rs.append(r)
    return [_dot(a, vb) for a, vb in zip(ws, vbs)], rs


def _attn_prompt_kernel(bias_ref, q_ref, k_ref, v_ref, lm_ref, o_ref, kbf, vbf, acc, rsum, zbuf, cbuf, tbuf,
                        *, tq, tk, ts):
    h = pl.program_id(1)
    qi = pl.program_id(2)
    nsub = tq // ts
    ndiag = tq // tk

    @pl.when(qi == 0)
    def _():
        kbf[...] = k_ref[0].astype(BF16)
        vbf[...] = v_ref[0].astype(BF16)

    q = (q_ref[0] * (DH_B ** -0.5)).astype(BF16)
    qs = [q[s * ts:(s + 1) * ts] for s in range(nsub)]
    bias = bias_ref[h]
    lm = lm_ref[...]
    row = lax.broadcasted_iota(jnp.int32, (ts, tk), 0)
    col = lax.broadcasted_iota(jnp.int32, (ts, tk), 1)
    started = set()

    def visit(k0, rel):
        kb = kbf[pl.ds(k0, tk), :]
        vb = vbf[pl.ds(k0, tk), :]
        subs, tiles, r0 = [], [], []
        for s in range(nsub):
            off = tk if rel is None else s * ts - rel
            if off + ts - 1 <= 0:
                continue
            valid = None if off >= tk else col < row + off
            subs.append(s)
            tiles.append((qs[s], kb, vb, valid))
            r0.append(rsum[s * ts:(s + 1) * ts, :] if s in started else jnp.zeros((ts, 1), F32))
        old = [acc[s * ts:(s + 1) * ts, :] if s in started else None for s in subs]
        pvs, rs = _sb_group(tiles, lm, bias, r0, False)
        for s, o, pv, r in zip(subs, old, pvs, rs):
            acc[s * ts:(s + 1) * ts, :] = pv if o is None else o + pv
            rsum[s * ts:(s + 1) * ts, :] = r
            started.add(s)

    q0 = qi * tq
    for d in reversed(range(ndiag)):
        visit(pl.multiple_of(q0 + d * tk, tk), d * tk)

    n_past = qi * ndiag

    def past_block(i):
        return pl.multiple_of(jnp.maximum(q0 - (i + 1) * tk, 0), tk)

    rows = [slice(s * ts, (s + 1) * ts) for s in range(nsub)]

    def qk(i):
        kb = kbf[pl.ds(past_block(i), tk), :]
        return [_dot_nt(qs[s], kb) + bias for s in range(nsub)]

    def finish_scores(zs, slot):
        cs, tots = _sb_cumulate(zs, [None] * nsub, lm, 1)
        for r, z, c, t in zip(rows, zs, cs, tots):
            zbuf[slot, r, :] = z
            cbuf[slot, r, :] = c
            tbuf[slot, r, :] = t

    def apply(i, slot):
        vb = vbf[pl.ds(past_block(i), tk), :]
        pvs, rs = _sb_apply([zbuf[slot, r, :] for r in rows], [cbuf[slot, r, :] for r in rows],
                            [tbuf[slot, r, :] for r in rows], [vb] * nsub, [None] * nsub,
                            [rsum[r, :] for r in rows], False)
        for r, pv, rn in zip(rows, pvs, rs):
            acc[r, :] += pv
            rsum[r, :] = rn

    @pl.when(n_past > 0)
    def _():
        finish_scores(qk(0), 0)

    def body(j, carry):
        for slot in range(2):
            i = 2 * j + slot
            z_next = qk(i + 1)
            apply(i, slot)
            finish_scores(z_next, 1 - slot)
        return carry

    lax.fori_loop(0, n_past // 2, body, 0)
    o_ref[0] = acc[...]


def _later_ones(n):
    r = lax.broadcasted_iota(jnp.int32, (n, n), 0)
    c = lax.broadcasted_iota(jnp.int32, (n, n), 1)
    return (r > c).astype(BF16)


def attn_prompt(h, sb_bias):
    b, l, _ = h.shape
    tq = min(512, l)
    tk = min(256, tq)
    ts = min(128, tq)
    assert l == tq or tq == 2 * tk
    qb = QKVB_OFF // DH_B
    kern = functools.partial(_attn_prompt_kernel, tq=tq, tk=tk, ts=ts)
    grid_spec = pltpu.PrefetchScalarGridSpec(
        num_scalar_prefetch=0,
        grid=(b, H_B, l // tq),
        in_specs=[pl.BlockSpec(memory_space=pltpu.SMEM),
                  pl.BlockSpec((1, tq, DH_B), lambda i, hh, j: (i, j, qb + hh)),
                  pl.BlockSpec((1, l, DH_B), lambda i, hh, j: (i, 0, qb + H_B + hh)),
                  pl.BlockSpec((1, l, DH_B), lambda i, hh, j: (i, 0, qb + 2 * H_B + hh)),
                  pl.BlockSpec((tk, tk), lambda i, hh, j: (0, 0))],
        out_specs=pl.BlockSpec((1, tq, DH_B), lambda i, hh, j: (i, j, hh)),
        scratch_shapes=[pltpu.VMEM((l, DH_B), BF16),
                        pltpu.VMEM((l, DH_B), BF16),
                        pltpu.VMEM((tq, DH_B), F32),
                        pltpu.VMEM((tq, 1), F32),
                        pltpu.VMEM((2, tq, tk), F32),
                        pltpu.VMEM((2, tq, tk), F32),
                        pltpu.VMEM((2, tq, 1), F32)])
    return pl.pallas_call(
        kern,
        grid_spec=grid_spec,
        out_shape=jax.ShapeDtypeStruct((b, l, W_B), F32),
        compiler_params=_cparams(("parallel", "parallel", "arbitrary")),
        name="attn_prompt",
    )(sb_bias, h, h, h, _later_ones(tk))


def _attn_sample_kernel(pt_ref, bias_ref, qkv_ref, *rest, seq, pages_per_step, page_group, n_steps):
    g_pages = pages_per_step
    k_refs = rest[:g_pages]
    v_refs = rest[g_pages:2 * g_pages]
    lm_ref = rest[2 * g_pages]
    lx_ref = rest[2 * g_pages + 1]
    o_ref = rest[2 * g_pages + 2]
    qall, acc, rsum = rest[2 * g_pages + 3:]
    s = pl.program_id(1)
    rows = H_B * seq
    pcols = PAGE_SIZE * H_B
    rid = lax.broadcasted_iota(jnp.int32, (rows, 1), 0)
    bias = jnp.zeros((rows, 1), F32)
    for hh in range(H_B):
        bias = jnp.where((rid >= hh * seq) & (rid < (hh + 1) * seq), bias_ref[hh], bias)

    @pl.when(s == 0)
    def _():
        qkv = qkv_ref[0]
        q = qkv[:, 0:W_B] * (DH_B ** -0.5)
        qall[...] = jnp.concatenate([q[:, hh * DH_B:(hh + 1) * DH_B] for hh in range(H_B)], axis=0).astype(BF16)
        lane = lax.broadcasted_iota(jnp.int32, (seq, W_B), 1)
        parts = [jnp.where((lane >= hh * DH_B) & (lane < (hh + 1) * DH_B), q, 0.0) for hh in range(H_B)]
        qbd = jnp.concatenate(parts, axis=0).astype(BF16)
        zrows = jnp.zeros((PAGE_SIZE - seq, W_B), F32)
        k_own = jnp.concatenate([qkv[:, W_B:2 * W_B], zrows], axis=0).astype(BF16)
        v_own = jnp.concatenate([qkv[:, 2 * W_B:3 * W_B], zrows], axis=0).astype(BF16)
        key = lax.broadcasted_iota(jnp.int32, (rows, PAGE_SIZE), 1)
        qpos = lax.broadcasted_iota(jnp.int32, (rows, PAGE_SIZE), 0) % seq
        pvs, rs = _sb_group([(qbd, k_own, v_own, key < qpos)], lm_ref[...], bias, jnp.zeros((rows, 1), F32), True)
        acc[...] = jnp.concatenate(
            [pvs[0][hh * seq:(hh + 1) * seq, hh * DH_B:(hh + 1) * DH_B] for hh in range(H_B)], axis=0)
        rsum[...] = rs[0]

    own = (lax.broadcasted_iota(jnp.int32, (rows, pcols), 1) % H_B
           == lax.broadcasted_iota(jnp.int32, (rows, pcols), 0) // seq)
    qa = qall[...]
    lx = lx_ref[...]
    r = rsum[...]
    total = acc[...]
    for g0 in reversed(range(0, g_pages, page_group)):
        tiles = [(qa, k_refs[gi][0, 0].astype(BF16), v_refs[gi][0, 0].astype(BF16), own)
                 for gi in reversed(range(g0, g0 + page_group))]
        pvs, rs = _sb_group(tiles, lx, bias, r, True, H_B)
        r = rs[-1]
        for pv in pvs:
            total = total + pv
    acc[...] = total
    rsum[...] = r

    @pl.when(s == n_steps - 1)
    def _():
        a = acc[...]
        o_ref[0] = jnp.concatenate([a[hh * seq:(hh + 1) * seq, :] for hh in range(H_B)], axis=1)


def attn_sample(h, cache_k, cache_v, page_table, sb_bias, layer):
    b, seq, _ = h.shape
    depth, n_pool = cache_k.shape[:2]
    n_pages = page_table.shape[0] // b
    g_pages = min(8, n_pages)
    n_steps = n_pages // g_pages
    rows = H_B * seq
    pcols = PAGE_SIZE * H_B
    ck = cache_k.reshape(depth, n_pool, pcols, DH_B)
    cv = cache_v.reshape(depth, n_pool, pcols, DH_B)
    kidx = lax.broadcasted_iota(jnp.int32, (pcols, pcols), 0) // H_B
    lexp = (kidx > kidx.T).astype(BF16)

    def page_map(gi):
        def index_map(i, s, pt):
            return (layer, pt[i * n_pages + (n_steps - 1 - s) * g_pages + gi], 0, 0)
        return index_map

    page_specs = [pl.BlockSpec((1, 1, pcols, DH_B), page_map(gi)) for gi in range(g_pages)]
    kern = functools.partial(_attn_sample_kernel, seq=seq, pages_per_step=g_pages, page_group=min(4, g_pages),
                             n_steps=n_steps)
    grid_spec = pltpu.PrefetchScalarGridSpec(
        num_scalar_prefetch=1,
        grid=(b, n_steps),
        in_specs=([pl.BlockSpec(memory_space=pltpu.SMEM),
                   pl.BlockSpec((1, seq, QKVB_W), lambda i, s, pt: (i, 0, QKVB_OFF // QKVB_W))]
                  + page_specs + page_specs
                  + [pl.BlockSpec((PAGE_SIZE, PAGE_SIZE), lambda i, s, pt: (0, 0)),
                     pl.BlockSpec((pcols, pcols), lambda i, s, pt: (0, 0))]),
        out_specs=pl.BlockSpec((1, seq, W_B), lambda i, s, pt: (i, 0, 0)),
        scratch_shapes=[pltpu.VMEM((rows, DH_B), BF16),
                        pltpu.VMEM((rows, DH_B), F32),
                        pltpu.VMEM((rows, 1), F32)])
    return pl.pallas_call(
        kern,
        grid_spec=grid_spec,
        out_shape=jax.ShapeDtypeStruct((b, seq, W_B), F32),
        compiler_params=_cparams(("parallel", "arbitrary")),
        name="attn_sample",
    )(page_table, sb_bias, h, *([ck] * g_pages), *([cv] * g_pages), _later_ones(PAGE_SIZE), lexp)


def _merge_kernel(x_ref, g_ref, c_ref, oa_ref, ob_ref, wpa_ref, wpb_ref, wpc_ref, wo_ref, conv_ref, ln_ref, sb0_ref,
                  o_ref, tail_ref, ext_scr, *, tm):
    t = pl.program_id(1)

    @pl.when(t == 0)
    def _():
        ext_scr[0:HALO, :] = sb0_ref[0]

    cc = c_ref[0]
    u = cc[:, 2 * W_C:] * cc[:, :W_C]
    ext_scr[HALO:HALO + tm, :] = u
    w = conv_ref[...]
    y = ext_scr[HALO - 2:HALO - 2 + tm, :] * w[0:1, :] + ext_scr[HALO - 1:HALO - 1 + tm, :] * w[1:2, :] + u * w[2:3, :]
    tail = ext_scr[tm:tm + HALO, :]
    ext_scr[0:HALO, :] = tail
    tail_ref[0] = tail
    o_c = cc[:, W_C:2 * W_C] * y

    g = g_ref[0]
    merged = (_sigmoid(g[:, :D_MODEL]) * _dot(oa_ref[0].astype(BF16), wpa_ref[...])
              + _sigmoid(g[:, D_MODEL:2 * D_MODEL]) * _dot(ob_ref[0].astype(BF16), wpb_ref[...])
              + _sigmoid(g[:, 2 * D_MODEL:]) * _dot(o_c.astype(BF16), wpc_ref[...]))
    r = DEEPNORM_ALPHA * x_ref[0] + _dot(merged.astype(BF16), wo_ref[...])
    o_ref[0] = _layer_norm(r, ln_ref[0:1, :], ln_ref[1:2, :])


def merge(x, h, o_a, o_b, wpa, wpb, wpc, wo, conv_c, ln, sb0):
    b, l, _ = x.shape
    tm = min(256, l)
    kern = functools.partial(_merge_kernel, tm=tm)
    const = lambda i, t: (0, 0)
    return pl.pallas_call(
        kern,
        grid=(b, l // tm),
        in_specs=[pl.BlockSpec((1, tm, D_MODEL), lambda i, t: (i, t, 0)),
                  pl.BlockSpec((1, tm, G_W), lambda i, t: (i, t, G_OFF // G_W)),
                  pl.BlockSpec((1, tm, C_W), lambda i, t: (i, t, C_OFF // C_W)),
                  pl.BlockSpec((1, tm, W_V_A), lambda i, t: (i, t, 0)),
                  pl.BlockSpec((1, tm, W_B), lambda i, t: (i, t, 0)),
                  pl.BlockSpec((W_V_A, D_MODEL), const),
                  pl.BlockSpec((W_B, D_MODEL), const),
                  pl.BlockSpec((W_C, D_MODEL), const),
                  pl.BlockSpec((D_MODEL, D_MODEL), const),
                  pl.BlockSpec((CONV_C, W_C), const),
                  pl.BlockSpec((2, D_MODEL), const),
                  pl.BlockSpec((1, HALO, W_C), lambda i, t: (i, 0, 0))],
        out_specs=[pl.BlockSpec((1, tm, D_MODEL), lambda i, t: (i, t, 0)),
                   pl.BlockSpec((1, HALO, W_C), lambda i, t: (i, 0, 0))],
        out_shape=[jax.ShapeDtypeStruct((b, l, D_MODEL), F32),
                   jax.ShapeDtypeStruct((b, HALO, W_C), F32)],
        scratch_shapes=[pltpu.VMEM((tm + HALO, W_C), F32)],
        compiler_params=_cparams(("parallel", "arbitrary")),
        name="merge",
    )(x, h, h, o_a, o_b, wpa, wpb, wpc, wo, conv_c, ln, sb0)


def _moe_kernel(x_ref, wr_ref, br_ref, wgu_ref, wdn_ref, ln_ref, o_ref, xb, gates, acc, *, tm):
    e = pl.program_id(1)
    lane = lax.broadcasted_iota(jnp.int32, (tm, LANES), 1)

    @pl.when(e == 0)
    def _():
        x = x_ref[...]
        xh, xm, xl = _split3(x)
        wh, wm, wl = _split3(wr_ref[...])
        logits = (_dot(xh, wh) + _dot(xh, wm) + _dot(xm, wh) + _dot(xh, wl) + _dot(xl, wh) + _dot(xm, wm)
                  + br_ref[...])
        lanef = lane.astype(F32)
        big = float(LANES)
        is_g = lane < N_GROUPS
        gl = jnp.where(is_g, logits, -jnp.inf)
        gmax = jnp.max(gl, axis=-1, keepdims=True)
        gsel = jnp.min(jnp.where(gl == gmax, lanef, big), axis=-1, keepdims=True)
        pg_sel = 1.0 / jnp.sum(jnp.where(is_g, jnp.exp(gl - gmax), 0.0), axis=-1, keepdims=True)
        lo = N_GROUPS + gsel * EXPERTS_PER_GROUP
        ev = jnp.where((lanef >= lo) & (lanef < lo + EXPERTS_PER_GROUP), logits, -jnp.inf)
        v1 = jnp.max(ev, axis=-1, keepdims=True)
        i1 = jnp.min(jnp.where(ev == v1, lanef, big), axis=-1, keepdims=True)
        ev2 = jnp.where(lanef == i1, -jnp.inf, ev)
        v2 = jnp.max(ev2, axis=-1, keepdims=True)
        i2 = jnp.min(jnp.where(ev2 == v2, lanef, big), axis=-1, keepdims=True)
        e2 = jnp.exp(v2 - v1)
        den = 1.0 + e2
        gates[...] = jnp.where(lanef == i1, pg_sel / den, jnp.where(lanef == i2, pg_sel * e2 / den, 0.0))
        xb[...] = xh
        acc[...] = jnp.zeros((tm, D_MODEL), F32)

    ge = jnp.sum(jnp.where(lane == N_GROUPS + e, gates[...], 0.0), axis=-1, keepdims=True)
    hh = _dot(xb[...], wgu_ref[0])
    act = _silu(hh[:, :D_EXPERT]) * hh[:, D_EXPERT:] * ge
    acc[...] += _dot(act.astype(BF16), wdn_ref[0])

    @pl.when(e == N_EXPERTS - 1)
    def _():
        r = DEEPNORM_ALPHA * x_ref[...] + acc[...]
        o_ref[...] = _layer_norm(r, ln_ref[0:1, :], ln_ref[1:2, :])


def moe(x2d, wr, br, wgu, wdn, ln):
    t_rows = x2d.shape[0]
    tm = min(1024, t_rows)
    kern = functools.partial(_moe_kernel, tm=tm)
    return pl.pallas_call(
        kern,
        grid=(t_rows // tm, N_EXPERTS),
        in_specs=[pl.BlockSpec((tm, D_MODEL), lambda i, e: (i, 0)),
                  pl.BlockSpec((D_MODEL, LANES), lambda i, e: (0, 0)),
                  pl.BlockSpec((1, LANES), lambda i, e: (0, 0)),
                  pl.BlockSpec((1, D_MODEL, 2 * D_EXPERT), lambda i, e: (e, 0, 0)),
                  pl.BlockSpec((1, D_EXPERT, D_MODEL), lambda i, e: (e, 0, 0)),
                  pl.BlockSpec((2, D_MODEL), lambda i, e: (0, 0))],
        out_specs=pl.BlockSpec((tm, D_MODEL), lambda i, e: (i, 0)),
        out_shape=jax.ShapeDtypeStruct((t_rows, D_MODEL), F32),
        scratch_shapes=[pltpu.VMEM((tm, D_MODEL), BF16),
                        pltpu.VMEM((tm, LANES), F32),
                        pltpu.VMEM((tm, D_MODEL), F32)],
        compiler_params=_cparams(("parallel", "arbitrary")),
        name="moe",
    )(x2d, wr, br, wgu, wdn, ln)


def _prep_layer(l, w_in, conv_a, a_log, dt_bias, norm_a, conv_c, w_pa, w_pb, w_pc, w_o, ln_g, ln_b,
                w_rg, b_rg, w_re, b_re, w_gu, w_down):
    wi = w_in[l]
    o_z = W_QKV_A
    o_ba = o_z + W_V_A
    o_qb = o_ba + 2 * H_A
    o_c = o_qb + 3 * W_B
    o_g = o_c + 3 * W_C
    w_main = jnp.concatenate([wi[:, o_g:], wi[:, :W_QKV_A], wi[:, o_c:o_g], wi[:, o_qb:o_c], wi[:, o_z:o_ba]],
                             axis=1).astype(BF16)
    w_ba = jnp.pad(wi[:, o_ba:o_qb], ((0, 0), (0, LANES - 2 * H_A))).astype(BF16)
    prm = jnp.zeros((SUBLANES, LANES), F32)
    prm = prm.at[0, H_A:2 * H_A].set(a_log[l]).at[1, H_A:2 * H_A].set(dt_bias[l]).at[2, :DV_A].set(norm_a[l])
    w_r = jnp.pad(jnp.concatenate([w_rg[l], w_re[l]], axis=1), ((0, 0), (0, LANES - N_GROUPS - N_EXPERTS)))
    b_r = jnp.pad(jnp.concatenate([b_rg[l], b_re[l]]), (0, LANES - N_GROUPS - N_EXPERTS)).reshape(1, LANES)
    return dict(
        w_main=w_main, w_ba=w_ba, conv_a=conv_a[l], prm=prm, conv_c=conv_c[l],
        w_pa=w_pa[l].astype(BF16), w_pb=w_pb[l].astype(BF16), w_pc=w_pc[l].astype(BF16), w_o=w_o[l].astype(BF16),
        ln0=jnp.stack([ln_g[l, 0], ln_b[l, 0]]), ln1=jnp.stack([ln_g[l, 1], ln_b[l, 1]]),
        w_r=w_r, b_r=b_r, w_gu=w_gu[l].astype(BF16), w_down=w_down[l].astype(BF16))


def _pad_tail(buf):
    return jnp.pad(buf, ((0, 0), (HALO - buf.shape[1], 0), (0, 0)))


def _layer(x, p, s0, db0, sb0, attn_fn):
    b, l, _ = x.shape
    h = in_proj(x.reshape(b * l, D_MODEL), p["w_main"]).reshape(b, l, H_COLS)
    o_a, s_new, dtail = delta_mixer(x, h, p["w_ba"], p["conv_a"], p["prm"], s0, _pad_tail(db0))
    o_b = attn_fn(h)
    x1, stail = merge(x, h, o_a, o_b, p["w_pa"], p["w_pb"], p["w_pc"], p["w_o"], p["conv_c"], p["ln0"], _pad_tail(sb0))
    x2 = moe(x1.reshape(b * l, D_MODEL), p["w_r"], p["b_r"], p["w_gu"], p["w_down"], p["ln1"]).reshape(b, l, D_MODEL)
    k_rows = h[:, :, QKVB_OFF + W_B:QKVB_OFF + 2 * W_B].reshape(b, l, H_B, DH_B)
    v_rows = h[:, :, QKVB_OFF + 2 * W_B:QKVB_OFF + 3 * W_B].reshape(b, l, H_B, DH_B)
    return x2, s_new, dtail[:, HALO - (CONV_A - 1):], stail[:, HALO - (CONV_C - 1):], k_rows, v_rows


def kernel(x_prompt, x_sample, cache_k, cache_v, state_delta, state_dconv, state_sconv, page_table, w_in, conv_a, a_log, dt_bias, norm_a, sb_bias, conv_c, w_pa, w_pb, w_pc, w_o, ln_g, ln_b, w_rg, b_rg, w_re, b_re, w_gu, w_down):
    bp = x_prompt.shape[0]
    depth = w_in.shape[0]
    pt =page_table.reshape(-1).astype(jnp.int32)
    xp, xs = x_prompt, x_sample
    outs_p = [[] for _ in range(5)]
    outs_s = [[] for _ in range(5)]
    for l in range(depth):
        p = _prep_layer(l, w_in, conv_a, a_log, dt_bias, norm_a, conv_c, w_pa, w_pb, w_pc, w_o, ln_g, ln_b,
                        w_rg, b_rg, w_re, b_re, w_gu, w_down)
        bias = sb_bias[l]
        res = _layer(xp, p,
                     jnp.zeros((bp, H_A, DK_A, DV_A), F32),
                     jnp.zeros((bp, CONV_A - 1, W_QKV_A), F32),
                     jnp.zeros((bp, CONV_C - 1, W_C), F32),
                     lambda h: attn_prompt(h, bias))
        xp = res[0]
        for acc_list, r in zip(outs_p, res[1:]):
            acc_list.append(r)
        res = _layer(xs, p, state_delta[l], state_dconv[l], state_sconv[l],
                     lambda h: attn_sample(h, cache_k, cache_v, pt, bias, l))
        xs = res[0]
        for acc_list, r in zip(outs_s, res[1:]):
            acc_list.append(r)
    return (xp, xs, *[jnp.stack(o) for o in outs_p], *[jnp.stack(o) for o in outs_s])
```

```python
import functools

import jax
import jax.numpy as jnp
from jax import lax
from jax.experimental import pallas as pl
from jax.experimental.pallas import tpu as pltpu

F32 = jnp.float32
BF16 = jnp.bfloat16

D_MODEL = 1024
DEPTH = 4
H_A = 4
DK_A = 128
DV_A = 128
CONV_A = 4
CHUNK_A = 64
H_B = 4
DH_B = 128
W_C = 512
CONV_C = 3
N_GROUPS = 4
EXPERTS_PER_GROUP = 4
N_EXPERTS = N_GROUPS * EXPERTS_PER_GROUP
D_EXPERT = 256
PAGE_SIZE = 128

W_QK_A = H_A * DK_A
W_V_A = H_A * DV_A
W_QKV_A = 2 * W_QK_A + W_V_A
W_B = H_B * DH_B

DEEPNORM_ALPHA = (2.0 * DEPTH) ** 0.25
LN_EPS = 1e-5
RMS_EPS = 1e-6

G_OFF, G_W = 0, 3 * D_MODEL
QKVA_OFF, QKVA_W = 3072, W_QKV_A
C_OFF, C_W = 4608, 3 * W_C
QKVB_OFF, QKVB_W = 6144, 3 * W_B
Z_OFF, Z_W = 7680, W_V_A
H_COLS = 8192
LANES = 128
SUBLANES = 8
HALO = SUBLANES

VMEM_LIMIT = 56 * 1024 * 1024


def _cparams(sem):
    return pltpu.CompilerParams(dimension_semantics=sem, vmem_limit_bytes=VMEM_LIMIT)


def _dot(a, b):
    return jnp.dot(a, b, preferred_element_type=F32)


def _dot_nt(a, b):
    return lax.dot_general(a, b, (((1,), (1,)), ((), ())), preferred_element_type=F32)


def _dot_tn(a, b):
    return lax.dot_general(a, b, (((0,), (0,)), ((), ())), preferred_element_type=F32)


def _split2(x):
    hi = x.astype(BF16)
    lo = (x - hi.astype(F32)).astype(BF16)
    return hi, lo


def _split3(x):
    hi = x.astype(BF16)
    r = x - hi.astype(F32)
    mid = r.astype(BF16)
    lo = (r - mid.astype(F32)).astype(BF16)
    return hi, mid, lo


def _mm_sel(sel, x):
    hi, mid, lo = _split3(x)
    return _dot(sel, hi) + _dot(sel, mid) + _dot(sel, lo)


def _neg_abs(x):
    bits = lax.bitcast_convert_type(x, jnp.uint32) | jnp.uint32(0x80000000)
    return lax.bitcast_convert_type(bits, F32)


def _softplus(x):
    return jnp.maximum(x, 0.0) + jnp.log1p(jnp.exp(-jnp.abs(x)))


def _sigmoid(x):
    return 1.0 / (1.0 + jnp.exp(-x))


def _silu(x):
    return x * _sigmoid(x)


def _layer_norm(r, g, b):
    mu = jnp.mean(r, axis=-1, keepdims=True)
    d = r - mu
    var = jnp.mean(d * d, axis=-1, keepdims=True)
    return d * lax.rsqrt(var + LN_EPS) * g + b


K_COL = QKVB_OFF + W_B
V_COL = QKVB_OFF + 2 * W_B


def _inproj_kernel(x_ref, w_ref, *refs, tm, tn):
    o_ref, k_ref, v_ref = refs[-3:]
    j = pl.program_id(1)
    o_ref[...] = _dot(x_ref[...].astype(BF16), w_ref[...])

    def rows_out(ref, col):
        @pl.when(j == col // tn)
        def _():
            for hh in range(H_B):
                c0 = col % tn + hh * DH_B
                ref[0, pl.ds(hh, tm, stride=H_B), :] = o_ref[:, c0:c0 + DH_B]

    rows_out(k_ref, K_COL)
    rows_out(v_ref, V_COL)


def in_proj(x2d, w, layer, depth, k_buf, v_buf):
    t_rows = x2d.shape[0]
    tm = min(1024, t_rows)
    tn = 1024
    assert K_COL % tn + W_B <= tn and V_COL % tn + W_B <= tn
    rows_shape = jax.ShapeDtypeStruct((depth, t_rows * H_B, DH_B), F32)
    rows_spec = pl.BlockSpec((1, tm * H_B, DH_B), lambda i, j: (layer, i, 0))
    carried = [] if k_buf is None else [k_buf, v_buf]
    return pl.pallas_call(
        functools.partial(_inproj_kernel, tm=tm, tn=tn),
        grid=(t_rows // tm, H_COLS // tn),
        in_specs=[pl.BlockSpec((tm, D_MODEL), lambda i, j: (i, 0)),
                  pl.BlockSpec((D_MODEL, tn), lambda i, j: (0, j))]
                 + [pl.BlockSpec(memory_space=pl.ANY)] * len(carried),
        out_specs=[pl.BlockSpec((tm, tn), lambda i, j: (i, j)), rows_spec, rows_spec],
        out_shape=[jax.ShapeDtypeStruct((t_rows, H_COLS), F32), rows_shape, rows_shape],
        input_output_aliases={2: 1, 3: 2} if carried else {},
        compiler_params=_cparams(("parallel", "arbitrary")),
        name="in_proj",
    )(x2d, w, *carried)


def _delta_kernel(x_ref, qkv_ref, z_ref, wba_ref, conv_ref, prm_ref, s0_ref, db0_ref,
                  o_ref, s_out_ref, tail_ref,
                  s_scr, ext_scr, q_scr, k_scr, v_scr, bg_scr, o_scr, u_scr, wq_scr, qk_scr, kd_scr, gl_scr, *, tc, cp):
    c = CHUNK_A
    t = pl.program_id(1)

    @pl.when(t == 0)
    def _():
        s_scr[...] = s0_ref[0]
        ext_scr[0:HALO, :] = db0_ref[0]

    u = qkv_ref[0]
    ext_scr[HALO:HALO + tc, :] = u
    w = conv_ref[...]
    y = (ext_scr[HALO - 3:HALO - 3 + tc, :] * w[0:1, :] + ext_scr[HALO - 2:HALO - 2 + tc, :] * w[1:2, :]
         + ext_scr[HALO - 1:HALO - 1 + tc, :] * w[2:3, :] + u * w[3:4, :])
    tail = ext_scr[tc:tc + HALO, :]
    ext_scr[0:HALO, :] = tail
    tail_ref[0] = tail
    y = _silu(y)

    ba = _dot(x_ref[0].astype(BF16), wba_ref[...])
    beta = _sigmoid(ba)
    g = -jnp.exp(prm_ref[0:1, :]) * _softplus(ba + prm_ref[1:2, :])
    lane = lax.broadcasted_iota(jnp.int32, (tc, LANES), 1)
    bg = jnp.where(lane < H_A, beta, g)

    if cp > tc:
        zpad = jnp.zeros((cp - tc, LANES), F32)
        q_scr[tc:cp, :] = jnp.zeros((cp - tc, W_QK_A), F32)
        k_scr[tc:cp, :] = jnp.zeros((cp - tc, W_QK_A), F32)
        v_scr[tc:cp, :] = jnp.zeros((cp - tc, W_V_A), F32)
        bg_scr[tc:cp, :] = zpad
    bg_scr[0:tc, :] = bg
    for h in range(H_A):
        qh = y[:, h * DK_A:(h + 1) * DK_A]
        kh = y[:, W_QK_A + h * DK_A:W_QK_A + (h + 1) * DK_A]
        qn = qh * lax.rsqrt(jnp.sum(qh * qh, axis=-1, keepdims=True) + RMS_EPS) * (DK_A ** -0.5)
        kn = kh * lax.rsqrt(jnp.sum(kh * kh, axis=-1, keepdims=True) + RMS_EPS)
        q_scr[0:tc, h * DK_A:(h + 1) * DK_A] = qn
        k_scr[0:tc, h * DK_A:(h + 1) * DK_A] = kn
    v_scr[0:tc, :] = y[:, 2 * W_QK_A:]

    ri = lax.broadcasted_iota(jnp.int32, (c, c), 0)
    ci = lax.broadcasted_iota(jnp.int32, (c, c), 1)
    tril = (ri >= ci).astype(BF16)
    triu_f = (ri <= ci).astype(F32)
    ones = jnp.ones((c, c), BF16)

    def prep_many(ins):
        qcs, kcs, vcs, betas, gs = zip(*ins)
        gbs = [jnp.broadcast_to(g_c, (c, LANES)) for g_c in gs]
        gcs = [_mm_sel(tril, gb) for gb in gbs]
        grows = [_mm_sel(ones, gb[:, :c] * triu_f) for gb in gbs]
        decs = [jnp.exp(jnp.where(ri >= ci, gc[:, :c] - gr, -jnp.inf)) for gc, gr in zip(gcs, grows)]
        kbs = [kc * b for kc, b in zip(kcs, betas)]
        kqs = [_dot_nt(jnp.concatenate([kb, qc], axis=0).astype(BF16), kc.astype(BF16))
               for kb, qc, kc in zip(kbs, qcs, kcs)]
        qks = [kq[c:] * dec for kq, dec in zip(kqs, decs)]
        egcs = [jnp.exp(gc) for gc in gcs]
        rhss = [jnp.concatenate([vc * b, kb * egc], axis=1) for vc, b, kb, egc in zip(vcs, betas, kbs, egcs)]
        ns = [-jnp.where(ri > ci, kq[:c] * dec, 0.0) for kq, dec in zip(kqs, decs)]
        yys = ns
        for _ in range(5):
            nbs = [n.astype(BF16) for n in ns]
            ns = [_dot(nb, nb) for nb in nbs]
            prods = [_dot(yy.astype(BF16), n.astype(BF16)) for yy, n in zip(yys, ns)]
            yys = [yy + n + p for yy, n, p in zip(yys, ns, prods)]
        sols = [rhs + _dot(yy.astype(BF16), rhs.astype(BF16)) for rhs, yy in zip(rhss, yys)]
        gls = [gc[c - 1:c, :] for gc in gcs]
        outs = []
        for sol, qc, kc, qk, egc, gc, gl in zip(sols, qcs, kcs, qks, egcs, gcs, gls):
            wq = jnp.concatenate([sol[:, DV_A:], qc * egc], axis=0).astype(BF16)
            k_dec = (kc * jnp.exp(gl - gc)).astype(BF16)
            outs.append((sol[:, :DV_A], wq, qk.astype(BF16), k_dec,
                         jnp.broadcast_to(jnp.exp(gl), (SUBLANES, LANES))))
        return outs

    def prep(it, carry):
        where, ins = [], []
        for j in range(prep_unroll):
            ic = it * prep_unroll + j
            r0 = pl.multiple_of(ic * c, c)
            bgc = bg_scr[pl.ds(r0, c), :]
            for h in range(H_A):
                where.append((ic, r0, h))
                ins.append((q_scr[pl.ds(r0, c), h * DK_A:(h + 1) * DK_A], k_scr[pl.ds(r0, c), h * DK_A:(h + 1) * DK_A],
                            v_scr[pl.ds(r0, c), h * DV_A:(h + 1) * DV_A], bgc[:, h:h + 1], bgc[:, H_A + h:H_A + h + 1]))
        outs = prep_many(ins)
        for (ic, r0, h), (uu, wq, qk, k_dec, egl) in zip(where, outs):
            r2 = pl.multiple_of(ic * 2 * c, 2 * c)
            r8 = pl.multiple_of(ic * SUBLANES, SUBLANES)
            u_scr[pl.ds(r0, c), h * DV_A:(h + 1) * DV_A] = uu
            wq_scr[pl.ds(r2, 2 * c), h * DK_A:(h + 1) * DK_A] = wq
            qk_scr[pl.ds(r0, c), h * LANES:h * LANES + c] = qk
            kd_scr[pl.ds(r0, c), h * DK_A:(h + 1) * DK_A] = k_dec
            gl_scr[pl.ds(r8, SUBLANES), h * LANES:(h + 1) * LANES] = egl
        return carry

    def scan(ic, carry):
        r0 = pl.multiple_of(ic * c, c)
        r2 = pl.multiple_of(ic * 2 * c, 2 * c)
        r8 = pl.multiple_of(ic * SUBLANES, SUBLANES)
        hs = range(H_A)
        us = [u_scr[pl.ds(r0, c), h * DV_A:(h + 1) * DV_A] for h in hs]
        wqs = [wq_scr[pl.ds(r2, 2 * c), h * DK_A:(h + 1) * DK_A] for h in hs]
        qks = [qk_scr[pl.ds(r0, c), h * LANES:h * LANES + c] for h in hs]
        kds = [kd_scr[pl.ds(r0, c), h * DK_A:(h + 1) * DK_A] for h in hs]
        egls = [gl_scr[pl.ds(r8, 1), h * LANES:(h + 1) * LANES] for h in hs]
        ss = [s_scr[h] for h in hs]
        wss = [_dot(wq, s.astype(BF16)) for wq, s in zip(wqs, ss)]
        vns = [(uu - ws[:c]).astype(BF16) for uu, ws in zip(us, wss)]
        ocs = [ws[c:] + _dot(qk, vn) for ws, qk, vn in zip(wss, qks, vns)]
        sns = [s * egl + _dot_tn(kd, vn) for s, egl, kd, vn in zip(ss, egls, kds, vns)]
        for h in hs:
            o_scr[pl.ds(r0, c), h * DV_A:(h + 1) * DV_A] = ocs[h]
            s_scr[h] = sns[h]
        return carry

    n_chunks = cp // c
    prep_unroll = 2 if n_chunks % 2 == 0 else 1
    lax.fori_loop(0, n_chunks // prep_unroll, prep, 0)
    lax.fori_loop(0, n_chunks, scan, 0)
    s_out_ref[0] = s_scr[...]

    z = z_ref[0]
    nw = prm_ref[2:3, :]
    for h in range(H_A):
        oh = o_scr[0:tc, h * DV_A:(h + 1) * DV_A]
        zh = z[:, h * DV_A:(h + 1) * DV_A]
        oh = oh * lax.rsqrt(jnp.mean(oh * oh, axis=-1, keepdims=True) + RMS_EPS)
        o_ref[0, :, h * DV_A:(h + 1) * DV_A] = oh * nw * _silu(zh)


def delta_mixer(x, h, wba, conv_a, prm, s0, db0):
    b, l, _ = x.shape
    tc = min(256, l)
    cp = max(tc, CHUNK_A)
    nt = l // tc
    kern = functools.partial(_delta_kernel, tc=tc, cp=cp)
    return pl.pallas_call(
        kern,
        grid=(b, nt),
        in_specs=[pl.BlockSpec((1, tc, D_MODEL), lambda i, t: (i, t, 0)),
                  pl.BlockSpec((1, tc, QKVA_W), lambda i, t: (i, t, QKVA_OFF // QKVA_W)),
                  pl.BlockSpec((1, tc, Z_W), lambda i, t: (i, t, Z_OFF // Z_W)),
                  pl.BlockSpec((D_MODEL, LANES), lambda i, t: (0, 0)),
                  pl.BlockSpec((CONV_A, W_QKV_A), lambda i, t: (0, 0)),
                  pl.BlockSpec((SUBLANES, LANES), lambda i, t: (0, 0)),
                  pl.BlockSpec((1, H_A, DK_A, DV_A), lambda i, t: (i, 0, 0, 0)),
                  pl.BlockSpec((1, HALO, W_QKV_A), lambda i, t: (i, 0, 0))],
        out_specs=[pl.BlockSpec((1, tc, W_V_A), lambda i, t: (i, t, 0)),
                   pl.BlockSpec((1, H_A, DK_A, DV_A), lambda i, t: (i, 0, 0, 0)),
                   pl.BlockSpec((1, HALO, W_QKV_A), lambda i, t: (i, 0, 0))],
        out_shape=[jax.ShapeDtypeStruct((b, l, W_V_A), F32),
                   jax.ShapeDtypeStruct((b, H_A, DK_A, DV_A), F32),
                   jax.ShapeDtypeStruct((b, HALO, W_QKV_A), F32)],
        scratch_shapes=[pltpu.VMEM((H_A, DK_A, DV_A), F32),
                        pltpu.VMEM((tc + HALO, W_QKV_A), F32),
                        pltpu.VMEM((cp, W_QK_A), F32),
                        pltpu.VMEM((cp, W_QK_A), F32),
                        pltpu.VMEM((cp, W_V_A), F32),
                        pltpu.VMEM((cp, LANES), F32),
                        pltpu.VMEM((cp, W_V_A), F32),
                        pltpu.VMEM((cp, W_V_A), F32),
                        pltpu.VMEM((2 * cp, W_QK_A), BF16),
                        pltpu.VMEM((cp, H_A * LANES), BF16),
                        pltpu.VMEM((cp, W_QK_A), BF16),
                        pltpu.VMEM((cp // CHUNK_A * SUBLANES, H_A * LANES), F32)],
        compiler_params=_cparams(("parallel", "arbitrary")),
        name="delta_mixer",
    )(x, h, h, wba, conv_a, prm, s0, db0)


def _sb_group(tiles, lm, bias, r0, chained, first_key_cols=1):
    zs, cs, tots = _sb_scores([(t[0], t[1], t[3]) for t in tiles], lm, bias, first_key_cols)
    return _sb_apply(zs, cs, tots, [t[2] for t in tiles], [t[3] for t in tiles], r0, chained)


def _sb_scores(tiles, lm, bias, first_key_cols):
    zs = [_dot_nt(q, kb) + bias for q, kb, _ in tiles]
    cs, tots = _sb_cumulate(zs, [t[2] for t in tiles], lm, first_key_cols)
    return zs, cs, tots


def _sb_cumulate(zs, valids, lm, first_key_cols):
    sps = [jnp.maximum(z, 0.0) + jnp.log(1.0 + jnp.exp(_neg_abs(z))) for z in zs]
    sps = [sp if v is None else jnp.where(v, sp, 0.0) for sp, v in zip(sps, valids)]
    later = [_dot(sp.astype(BF16), lm) for sp in sps]
    tots = [lt[:, 0:1] + jnp.sum(sp[:, 0:first_key_cols], axis=-1, keepdims=True) for lt, sp in zip(later, sps)]
    return [sp + lt for sp, lt in zip(sps, later)], tots


def _sb_apply(zs, cs, tots, vbs, valids, r0, chained):
    rs, ws = [], []
    r = r0
    for i, (z, c, tot, valid) in enumerate(zip(zs, cs, tots, valids)):
        r_prev = r if chained else r0[i]
        a = jnp.exp(z - c - r_prev)
        if valid is not None:
            a = jnp.where(valid, a, 0.0)
        ws.append(a.astype(BF16))
        r = r_prev + tot
        rs.append(r)
    return [_dot(a, vb) for a, vb in zip(ws, vbs)], rs


def _attn_prompt_kernel(bias_ref, q_ref, k_ref, v_ref, lm_ref, o_ref, kbf, vbf, acc, rsum, zbuf, cbuf, tbuf,
                        *, tq, tk, ts):
    h = pl.program_id(1)
    qi = pl.program_id(2)
    nsub = tq // ts
    ndiag = tq // tk

    @pl.when(qi == 0)
    def _():
        kbf[...] = k_ref[0].astype(BF16)
        vbf[...] = v_ref[0].astype(BF16)

    q = (q_ref[0] * (DH_B ** -0.5)).astype(BF16)
    qs = [q[s * ts:(s + 1) * ts] for s in range(nsub)]
    bias = bias_ref[h]
    lm = lm_ref[...]
    row = lax.broadcasted_iota(jnp.int32, (ts, tk), 0)
    col = lax.broadcasted_iota(jnp.int32, (ts, tk), 1)
    started = set()

    def visit(k0, rel):
        kb = kbf[pl.ds(k0, tk), :]
        vb = vbf[pl.ds(k0, tk), :]
        subs, tiles, r0 = [], [], []
        for s in range(nsub):
            off = tk if rel is None else s * ts - rel
            if off + ts - 1 <= 0:
                continue
            valid = None if off >= tk else col < row + off
            subs.append(s)
            tiles.append((qs[s], kb, vb, valid))
            r0.append(rsum[s * ts:(s + 1) * ts, :] if s in started else jnp.zeros((ts, 1), F32))
        old = [acc[s * ts:(s + 1) * ts, :] if s in started else None for s in subs]
        pvs, rs = _sb_group(tiles, lm, bias, r0, False)
        for s, o, pv, r in zip(subs, old, pvs, rs):
            acc[s * ts:(s + 1) * ts, :] = pv if o is None else o + pv
            rsum[s * ts:(s + 1) * ts, :] = r
            started.add(s)

    q0 = qi * tq
    for d in reversed(range(ndiag)):
        visit(pl.multiple_of(q0 + d * tk, tk), d * tk)

    n_past = qi * ndiag

    def past_block(i):
        return pl.multiple_of(jnp.maximum(q0 - (i + 1) * tk, 0), tk)

    rows = [slice(s * ts, (s + 1) * ts) for s in range(nsub)]

    def qk(i):
        kb = kbf[pl.ds(past_block(i), tk), :]
        return [_dot_nt(qs[s], kb) + bias for s in range(nsub)]

    def finish_scores(zs, slot):
        cs, tots = _sb_cumulate(zs, [None] * nsub, lm, 1)
        for r, z, c, t in zip(rows, zs, cs, tots):
            zbuf[slot, r, :] = z
            cbuf[slot, r, :] = c
            tbuf[slot, r, :] = t

    def apply(i, slot):
        vb = vbf[pl.ds(past_block(i), tk), :]
        pvs, rs = _sb_apply([zbuf[slot, r, :] for r in rows], [cbuf[slot, r, :] for r in rows],
                            [tbuf[slot, r, :] for r in rows], [vb] * nsub, [None] * nsub,
                            [rsum[r, :] for r in rows], False)
        for r, pv, rn in zip(rows, pvs, rs):
            acc[r, :] += pv
            rsum[r, :] = rn

    @pl.when(n_past > 0)
    def _():
        finish_scores(qk(0), 0)

    def body(j, carry):
        for slot in range(2):
            i = 2 * j + slot
            z_next = qk(i + 1)
            apply(i, slot)
            finish_scores(z_next, 1 - slot)
        return carry

    lax.fori_loop(0, n_past // 2, body, 0)
    o_ref[0] = acc[...]


def _later_ones(n):
    r = lax.broadcasted_iota(jnp.int32, (n, n), 0)
    c = lax.broadcasted_iota(jnp.int32, (n, n), 1)
    return (r > c).astype(BF16)


def attn_prompt(h, sb_bias):
    b, l, _ = h.shape
    tq = min(512, l)
    tk = min(256, tq)
    ts = min(128, tq)
    assert l == tq or tq == 2 * tk
    qb = QKVB_OFF // DH_B
    kern = functools.partial(_attn_prompt_kernel, tq=tq, tk=tk, ts=ts)
    grid_spec = pltpu.PrefetchScalarGridSpec(
        num_scalar_prefetch=0,
        grid=(b, H_B, l // tq),
        in_specs=[pl.BlockSpec(memory_space=pltpu.SMEM),
                  pl.BlockSpec((1, tq, DH_B), lambda i, hh, j: (i, j, qb + hh)),
                  pl.BlockSpec((1, l, DH_B), lambda i, hh, j: (i, 0, qb + H_B + hh)),
                  pl.BlockSpec((1, l, DH_B), lambda i, hh, j: (i, 0, qb + 2 * H_B + hh)),
                  pl.BlockSpec((tk, tk), lambda i, hh, j: (0, 0))],
        out_specs=pl.BlockSpec((1, tq, DH_B), lambda i, hh, j: (i, j, hh)),
        scratch_shapes=[pltpu.VMEM((l, DH_B), BF16),
                        pltpu.VMEM((l, DH_B), BF16),
                        pltpu.VMEM((tq, DH_B), F32),
                        pltpu.VMEM((tq, 1), F32),
                        pltpu.VMEM((2, tq, tk), F32),
                        pltpu.VMEM((2, tq, tk), F32),
                        pltpu.VMEM((2, tq, 1), F32)])
    return pl.pallas_call(
        kern,
        grid_spec=grid_spec,
        out_shape=jax.ShapeDtypeStruct((b, l, W_B), F32),
        compiler_params=_cparams(("parallel", "parallel", "arbitrary")),
        name="attn_prompt",
    )(sb_bias, h, h, h, _later_ones(tk))


def _attn_sample_kernel(pt_ref, bias_ref, qkv_ref, *rest, seq, pages_per_step, page_group, n_steps):
    g_pages = pages_per_step
    k_refs = rest[:g_pages]
    v_refs = rest[g_pages:2 * g_pages]
    lm_ref = rest[2 * g_pages]
    lx_ref = rest[2 * g_pages + 1]
    o_ref = rest[2 * g_pages + 2]
    qall, acc, rsum = rest[2 * g_pages + 3:]
    s = pl.program_id(1)
    rows = H_B * seq
    pcols = PAGE_SIZE * H_B
    rid = lax.broadcasted_iota(jnp.int32, (rows, 1), 0)
    bias = jnp.zeros((rows, 1), F32)
    for hh in range(H_B):
        bias = jnp.where((rid >= hh * seq) & (rid < (hh + 1) * seq), bias_ref[hh], bias)

    @pl.when(s == 0)
    def _():
        qkv = qkv_ref[0]
        q = qkv[:, 0:W_B] * (DH_B ** -0.5)
        qall[...] = jnp.concatenate([q[:, hh * DH_B:(hh + 1) * DH_B] for hh in range(H_B)], axis=0).astype(BF16)
        lane = lax.broadcasted_iota(jnp.int32, (seq, W_B), 1)
        parts = [jnp.where((lane >= hh * DH_B) & (lane < (hh + 1) * DH_B), q, 0.0) for hh in range(H_B)]
        qbd = jnp.concatenate(parts, axis=0).astype(BF16)
        zrows = jnp.zeros((PAGE_SIZE - seq, W_B), F32)
        k_own = jnp.concatenate([qkv[:, W_B:2 * W_B], zrows], axis=0).astype(BF16)
        v_own = jnp.concatenate([qkv[:, 2 * W_B:3 * W_B], zrows], axis=0).astype(BF16)
        key = lax.broadcasted_iota(jnp.int32, (rows, PAGE_SIZE), 1)
        qpos = lax.broadcasted_iota(jnp.int32, (rows, PAGE_SIZE), 0) % seq
        pvs, rs = _sb_group([(qbd, k_own, v_own, key < qpos)], lm_ref[...], bias, jnp.zeros((rows, 1), F32), True)
        acc[...] = jnp.concatenate(
            [pvs[0][hh * seq:(hh + 1) * seq, hh * DH_B:(hh + 1) * DH_B] for hh in range(H_B)], axis=0)
        rsum[...] = rs[0]

    own = (lax.broadcasted_iota(jnp.int32, (rows, pcols), 1) % H_B
           == lax.broadcasted_iota(jnp.int32, (rows, pcols), 0) // seq)
    qa = qall[...]
    lx = lx_ref[...]
    r = rsum[...]
    total = acc[...]
    for g0 in reversed(range(0, g_pages, page_group)):
        tiles = [(qa, k_refs[gi][0, 0].astype(BF16), v_refs[gi][0, 0].astype(BF16), own)
                 for gi in reversed(range(g0, g0 + page_group))]
        pvs, rs = _sb_group(tiles, lx, bias, r, True, H_B)
        r = rs[-1]
        for pv in pvs:
            total = total + pv
    acc[...] = total
    rsum[...] = r

    @pl.when(s == n_steps - 1)
    def _():
        a = acc[...]
        o_ref[0] = jnp.concatenate([a[hh * seq:(hh + 1) * seq, :] for hh in range(H_B)], axis=1)


def attn_sample(h, cache_k, cache_v, page_table, sb_bias, layer):
    b, seq, _ = h.shape
    depth, n_pool = cache_k.shape[:2]
    n_pages = page_table.shape[0] // b
    g_pages = min(16, n_pages)
    n_steps = n_pages // g_pages
    rows = H_B * seq
    pcols = PAGE_SIZE * H_B
    ck = cache_k.reshape(depth, n_pool, pcols, DH_B)
    cv = cache_v.reshape(depth, n_pool, pcols, DH_B)
    kidx = lax.broadcasted_iota(jnp.int32, (pcols, pcols), 0) // H_B
    lexp = (kidx > kidx.T).astype(BF16)

    def page_map(gi):
        def index_map(i, s, pt):
            return (layer, pt[i * n_pages + (n_steps - 1 - s) * g_pages + gi], 0, 0)
        return index_map

    page_specs = [pl.BlockSpec((1, 1, pcols, DH_B), page_map(gi)) for gi in range(g_pages)]
    kern = functools.partial(_attn_sample_kernel, seq=seq, pages_per_step=g_pages, page_group=min(4, g_pages),
                             n_steps=n_steps)
    grid_spec = pltpu.PrefetchScalarGridSpec(
        num_scalar_prefetch=1,
        grid=(b, n_steps),
        in_specs=([pl.BlockSpec(memory_space=pltpu.SMEM),
                   pl.BlockSpec((1, seq, QKVB_W), lambda i, s, pt: (i, 0, QKVB_OFF // QKVB_W))]
                  + page_specs + page_specs
                  + [pl.BlockSpec((PAGE_SIZE, PAGE_SIZE), lambda i, s, pt: (0, 0)),
                     pl.BlockSpec((pcols, pcols), lambda i, s, pt: (0, 0))]),
        out_specs=pl.BlockSpec((1, seq, W_B), lambda i, s, pt: (i, 0, 0)),
        scratch_shapes=[pltpu.VMEM((rows, DH_B), BF16),
                        pltpu.VMEM((rows, DH_B), F32),
                        pltpu.VMEM((rows, 1), F32)])
    return pl.pallas_call(
        kern,
        grid_spec=grid_spec,
        out_shape=jax.ShapeDtypeStruct((b, seq, W_B), F32),
        compiler_params=_cparams(("parallel", "arbitrary")),
        name="attn_sample",
    )(page_table, sb_bias, h, *([ck] * g_pages), *([cv] * g_pages), _later_ones(PAGE_SIZE), lexp)


def _merge_kernel(x_ref, g_ref, c_ref, oa_ref, ob_ref, wpa_ref, wpb_ref, wpc_ref, wo_ref, conv_ref, ln_ref, sb0_ref,
                  o_ref, tail_ref, ext_scr, *, tm):
    t = pl.program_id(1)

    @pl.when(t == 0)
    def _():
        ext_scr[0:HALO, :] = sb0_ref[0]

    cc = c_ref[0]
    u = cc[:, 2 * W_C:] * cc[:, :W_C]
    ext_scr[HALO:HALO + tm, :] = u
    w = conv_ref[...]
    y = ext_scr[HALO - 2:HALO - 2 + tm, :] * w[0:1, :] + ext_scr[HALO - 1:HALO - 1 + tm, :] * w[1:2, :] + u * w[2:3, :]
    tail = ext_scr[tm:tm + HALO, :]
    ext_scr[0:HALO, :] = tail
    tail_ref[0] = tail
    o_c = cc[:, W_C:2 * W_C] * y

    g = g_ref[0]
    merged = (_sigmoid(g[:, :D_MODEL]) * _dot(oa_ref[0].astype(BF16), wpa_ref[...])
              + _sigmoid(g[:, D_MODEL:2 * D_MODEL]) * _dot(ob_ref[0].astype(BF16), wpb_ref[...])
              + _sigmoid(g[:, 2 * D_MODEL:]) * _dot(o_c.astype(BF16), wpc_ref[...]))
    r = DEEPNORM_ALPHA * x_ref[0] + _dot(merged.astype(BF16), wo_ref[...])
    o_ref[0] = _layer_norm(r, ln_ref[0:1, :], ln_ref[1:2, :])


def merge(x, h, o_a, o_b, wpa, wpb, wpc, wo, conv_c, ln, sb0):
    b, l, _ = x.shape
    tm = min(256, l)
    kern = functools.partial(_merge_kernel, tm=tm)
    const = lambda i, t: (0, 0)
    return pl.pallas_call(
        kern,
        grid=(b, l // tm),
        in_specs=[pl.BlockSpec((1, tm, D_MODEL), lambda i, t: (i, t, 0)),
                  pl.BlockSpec((1, tm, G_W), lambda i, t: (i, t, G_OFF // G_W)),
                  pl.BlockSpec((1, tm, C_W), lambda i, t: (i, t, C_OFF // C_W)),
                  pl.BlockSpec((1, tm, W_V_A), lambda i, t: (i, t, 0)),
                  pl.BlockSpec((1, tm, W_B), lambda i, t: (i, t, 0)),
                  pl.BlockSpec((W_V_A, D_MODEL), const),
                  pl.BlockSpec((W_B, D_MODEL), const),
                  pl.BlockSpec((W_C, D_MODEL), const),
                  pl.BlockSpec((D_MODEL, D_MODEL), const),
                  pl.BlockSpec((CONV_C, W_C), const),
                  pl.BlockSpec((2, D_MODEL), const),
                  pl.BlockSpec((1, HALO, W_C), lambda i, t: (i, 0, 0))],
        out_specs=[pl.BlockSpec((1, tm, D_MODEL), lambda i, t: (i, t, 0)),
                   pl.BlockSpec((1, HALO, W_C), lambda i, t: (i, 0, 0))],
        out_shape=[jax.ShapeDtypeStruct((b, l, D_MODEL), F32),
                   jax.ShapeDtypeStruct((b, HALO, W_C), F32)],
        scratch_shapes=[pltpu.VMEM((tm + HALO, W_C), F32)],
        compiler_params=_cparams(("parallel", "arbitrary")),
        name="merge",
    )(x, h, h, o_a, o_b, wpa, wpb, wpc, wo, conv_c, ln, sb0)


def _moe_kernel(x_ref, wr_ref, br_ref, wgu_ref, wdn_ref, ln_ref, o_ref, xb, gates, acc, *, tm):
    e = pl.program_id(1)
    lane = lax.broadcasted_iota(jnp.int32, (tm, LANES), 1)

    @pl.when(e == 0)
    def _():
        x = x_ref[...]
        xh, xl = _split2(x)
        wh, wl = _split2(wr_ref[...])
        logits = _dot(xh, wh) + _dot(xh, wl) + _dot(xl, wh) + br_ref[...]
        lanef = lane.astype(F32)
        big = float(LANES)
        is_g = lane < N_GROUPS
        gl = jnp.where(is_g, logits, -jnp.inf)
        gmax = jnp.max(gl, axis=-1, keepdims=True)
        gsel = jnp.min(jnp.where(gl == gmax, lanef, big), axis=-1, keepdims=True)
        pg_sel = 1.0 / jnp.sum(jnp.where(is_g, jnp.exp(gl - gmax), 0.0), axis=-1, keepdims=True)
        lo = N_GROUPS + gsel * EXPERTS_PER_GROUP
        ev = jnp.where((lanef >= lo) & (lanef < lo + EXPERTS_PER_GROUP), logits, -jnp.inf)
        v1 = jnp.max(ev, axis=-1, keepdims=True)
        i1 = jnp.min(jnp.where(ev == v1, lanef, big), axis=-1, keepdims=True)
        ev2 = jnp.where(lanef == i1, -jnp.inf, ev)
        v2 = jnp.max(ev2, axis=-1, keepdims=True)
        i2 = jnp.min(jnp.where(ev2 == v2, lanef, big), axis=-1, keepdims=True)
        e2 = jnp.exp(v2 - v1)
        den = 1.0 + e2
        gates[...] = jnp.where(lanef == i1, pg_sel / den, jnp.where(lanef == i2, pg_sel * e2 / den, 0.0))
        xb[...] = xh
        acc[...] = jnp.zeros((tm, D_MODEL), F32)

    ge = jnp.sum(jnp.where(lane == N_GROUPS + e, gates[...], 0.0), axis=-1, keepdims=True)
    hh = _dot(xb[...], wgu_ref[0])
    act = _silu(hh[:, :D_EXPERT]) * hh[:, D_EXPERT:] * ge
    acc[...] += _dot(act.astype(BF16), wdn_ref[0])

    @pl.when(e == N_EXPERTS - 1)
    def _():
        r = DEEPNORM_ALPHA * x_ref[...] + acc[...]
        o_ref[...] = _layer_norm(r, ln_ref[0:1, :], ln_ref[1:2, :])


def moe(x2d, wr, br, wgu, wdn, ln):
    t_rows = x2d.shape[0]
    tm = min(1024, t_rows)
    kern = functools.partial(_moe_kernel, tm=tm)
    return pl.pallas_call(
        kern,
        grid=(t_rows // tm, N_EXPERTS),
        in_specs=[pl.BlockSpec((tm, D_MODEL), lambda i, e: (i, 0)),
                  pl.BlockSpec((D_MODEL, LANES), lambda i, e: (0, 0)),
                  pl.BlockSpec((1, LANES), lambda i, e: (0, 0)),
                  pl.BlockSpec((1, D_MODEL, 2 * D_EXPERT), lambda i, e: (e, 0, 0)),
                  pl.BlockSpec((1, D_EXPERT, D_MODEL), lambda i, e: (e, 0, 0)),
                  pl.BlockSpec((2, D_MODEL), lambda i, e: (0, 0))],
        out_specs=pl.BlockSpec((tm, D_MODEL), lambda i, e: (i, 0)),
        out_shape=jax.ShapeDtypeStruct((t_rows, D_MODEL), F32),
        scratch_shapes=[pltpu.VMEM((tm, D_MODEL), BF16),
                        pltpu.VMEM((tm, LANES), F32),
                        pltpu.VMEM((tm, D_MODEL), F32)],
        compiler_params=_cparams(("parallel", "arbitrary")),
        name="moe",
    )(x2d, wr, br, wgu, wdn, ln)


def _prep_layer(l, w_in, conv_a, a_log, dt_bias, norm_a, conv_c, w_pa, w_pb, w_pc, w_o, ln_g, ln_b,
                w_rg, b_rg, w_re, b_re, w_gu, w_down):
    wi = w_in[l]
    o_z = W_QKV_A
    o_ba = o_z + W_V_A
    o_qb = o_ba + 2 * H_A
    o_c = o_qb + 3 * W_B
    o_g = o_c + 3 * W_C
    w_main = jnp.concatenate([wi[:, o_g:], wi[:, :W_QKV_A], wi[:, o_c:o_g], wi[:, o_qb:o_c], wi[:, o_z:o_ba]],
                             axis=1).astype(BF16)
    w_ba = jnp.pad(wi[:, o_ba:o_qb], ((0, 0), (0, LANES - 2 * H_A))).astype(BF16)
    prm = jnp.zeros((SUBLANES, LANES), F32)
    prm = prm.at[0, H_A:2 * H_A].set(a_log[l]).at[1, H_A:2 * H_A].set(dt_bias[l]).at[2, :DV_A].set(norm_a[l])
    w_r = jnp.pad(jnp.concatenate([w_rg[l], w_re[l]], axis=1), ((0, 0), (0, LANES - N_GROUPS - N_EXPERTS)))
    b_r = jnp.pad(jnp.concatenate([b_rg[l], b_re[l]]), (0, LANES - N_GROUPS - N_EXPERTS)).reshape(1, LANES)
    return dict(
        w_main=w_main, w_ba=w_ba, conv_a=conv_a[l], prm=prm, conv_c=conv_c[l],
        w_pa=w_pa[l].astype(BF16), w_pb=w_pb[l].astype(BF16), w_pc=w_pc[l].astype(BF16), w_o=w_o[l].astype(BF16),
        ln0=jnp.stack([ln_g[l, 0], ln_b[l, 0]]), ln1=jnp.stack([ln_g[l, 1], ln_b[l, 1]]),
        w_r=w_r, b_r=b_r, w_gu=w_gu[l].astype(BF16), w_down=w_down[l].astype(BF16))


def _pad_tail(buf):
    return jnp.pad(buf, ((0, 0), (HALO - buf.shape[1], 0), (0, 0)))


def _layer(x, p, s0, db0, sb0, attn_fn, layer, depth, k_buf, v_buf):
    b, l, _ = x.shape
    h, k_buf, v_buf = in_proj(x.reshape(b * l, D_MODEL), p["w_main"], layer, depth, k_buf, v_buf)
    h = h.reshape(b, l, H_COLS)
    o_a, s_new, dtail = delta_mixer(x, h, p["w_ba"], p["conv_a"], p["prm"], s0, _pad_tail(db0))
    o_b = attn_fn(h)
    x1, stail = merge(x, h, o_a, o_b, p["w_pa"], p["w_pb"], p["w_pc"], p["w_o"], p["conv_c"], p["ln0"], _pad_tail(sb0))
    x2 = moe(x1.reshape(b * l, D_MODEL), p["w_r"], p["b_r"], p["w_gu"], p["w_down"], p["ln1"]).reshape(b, l, D_MODEL)
    return (x2, s_new, dtail[:, HALO - (CONV_A - 1):], stail[:, HALO - (CONV_C - 1):]), k_buf, v_buf


def kernel(x_prompt, x_sample, cache_k, cache_v, state_delta, state_dconv, state_sconv, page_table, w_in, conv_a, a_log, dt_bias, norm_a, sb_bias, conv_c, w_pa, w_pb, w_pc, w_o, ln_g, ln_b, w_rg, b_rg, w_re, b_re, w_gu, w_down):
    bp, lp, _ = x_prompt.shape
    bs, ls, _ = x_sample.shape
    depth = w_in.shape[0]
    pt = page_table.reshape(-1).astype(jnp.int32)
    xp, xs = x_prompt, x_sample
    outs_p = [[] for _ in range(3)]
    outs_s = [[] for _ in range(3)]
    pk = pv = sk = sv = None
    for l in range(depth):
        p = _prep_layer(l, w_in, conv_a, a_log, dt_bias, norm_a, conv_c, w_pa, w_pb, w_pc, w_o, ln_g, ln_b,
                        w_rg, b_rg, w_re, b_re, w_gu, w_down)
        bias = sb_bias[l]
        res, pk, pv = _layer(xp, p,
                             jnp.zeros((bp, H_A, DK_A, DV_A), F32),
                             jnp.zeros((bp, CONV_A - 1, W_QKV_A), F32),
                             jnp.zeros((bp, CONV_C - 1, W_C), F32),
                             lambda h: attn_prompt(h, bias), l, depth, pk, pv)
        xp = res[0]
        for acc_list, r in zip(outs_p, res[1:]):
            acc_list.append(r)
        res, sk, sv = _layer(xs, p, state_delta[l], state_dconv[l], state_sconv[l],
                             lambda h: attn_sample(h, cache_k, cache_v, pt, bias, l), l, depth, sk, sv)
        xs = res[0]
        for acc_list, r in zip(outs_s, res[1:]):
            acc_list.append(r)
    return (xp, xs, *[jnp.stack(o) for o in outs_p],
            pk.reshape(depth, bp, lp, H_B, DH_B), pv.reshape(depth, bp, lp, H_B, DH_B),
            *[jnp.stack(o) for o in outs_s],
            sk.reshape(depth, bs, ls, H_B, DH_B), sv.reshape(depth, bs, ls, H_B, DH_B))
```

```python
import functools

import jax
import jax.numpy as jnp
from jax import lax
from jax.experimental import pallas as pl
from jax.experimental.pallas import tpu as pltpu

F32 = jnp.float32
BF16 = jnp.bfloat16

D_MODEL = 1024
DEPTH = 4
H_A = 4
DK_A = 128
DV_A = 128
CONV_A = 4
CHUNK_A = 64
H_B = 4
DH_B = 128
W_C = 512
CONV_C = 3
N_GROUPS = 4
EXPERTS_PER_GROUP = 4
N_EXPERTS = N_GROUPS * EXPERTS_PER_GROUP
D_EXPERT = 256
PAGE_SIZE = 128

W_QK_A = H_A * DK_A
W_V_A = H_A * DV_A
W_QKV_A = 2 * W_QK_A + W_V_A
W_B = H_B * DH_B

DEEPNORM_ALPHA = (2.0 * DEPTH) ** 0.25
LN_EPS = 1e-5
RMS_EPS = 1e-6

G_OFF, G_W = 0, 3 * D_MODEL
QKVA_OFF, QKVA_W = 3072, W_QKV_A
C_OFF, C_W = 4608, 3 * W_C
QKVB_OFF, QKVB_W = 6144, 3 * W_B
Z_OFF, Z_W = 7680, W_V_A
H_COLS = 8192
LANES = 128
SUBLANES = 8
HALO = SUBLANES
PREP_GROUP = 16

VMEM_LIMIT = 56 * 1024 * 1024


def _cparams(sem):
    return pltpu.CompilerParams(dimension_semantics=sem, vmem_limit_bytes=VMEM_LIMIT)


def _dot(a, b):
    return jnp.dot(a, b, preferred_element_type=F32)


def _dot_nt(a, b):
    return lax.dot_general(a, b, (((1,), (1,)), ((), ())), preferred_element_type=F32)


def _dot_tn(a, b):
    return lax.dot_general(a, b, (((0,), (0,)), ((), ())), preferred_element_type=F32)


def _split2(x):
    hi = x.astype(BF16)
    lo = (x - hi.astype(F32)).astype(BF16)
    return hi, lo


def _split3(x):
    hi = x.astype(BF16)
    r = x - hi.astype(F32)
    mid = r.astype(BF16)
    lo = (r - mid.astype(F32)).astype(BF16)
    return hi, mid, lo


def _mm_sel(sel, x):
    hi, mid, lo = _split3(x)
    return _dot(sel, hi) + _dot(sel, mid) + _dot(sel, lo)


def _neg_abs(x):
    bits = lax.bitcast_convert_type(x, jnp.uint32) | jnp.uint32(0x80000000)
    return lax.bitcast_convert_type(bits, F32)


def _softplus(x):
    return jnp.maximum(x, 0.0) + jnp.log1p(jnp.exp(-jnp.abs(x)))


def _sigmoid(x):
    return 1.0 / (1.0 + jnp.exp(-x))


def _silu(x):
    return x * _sigmoid(x)


def _layer_norm(r, g, b):
    mu = jnp.mean(r, axis=-1, keepdims=True)
    d = r - mu
    var = jnp.mean(d * d, axis=-1, keepdims=True)
    return d * lax.rsqrt(var + LN_EPS) * g + b


K_COL = QKVB_OFF + W_B
V_COL = QKVB_OFF + 2 * W_B


def _inproj_kernel(x_ref, w_ref, *refs, tm, tn):
    o_ref, k_ref, v_ref = refs[-3:]
    j = pl.program_id(1)
    o_ref[...] = _dot(x_ref[...].astype(BF16), w_ref[...])

    def rows_out(ref, col):
        @pl.when(j == col // tn)
        def _():
            for hh in range(H_B):
                c0 = col % tn + hh * DH_B
                ref[0, pl.ds(hh, tm, stride=H_B), :] = o_ref[:, c0:c0 + DH_B]

    rows_out(k_ref, K_COL)
    rows_out(v_ref, V_COL)


def in_proj(x2d, w, layer, depth, k_buf, v_buf):
    t_rows = x2d.shape[0]
    tm = min(1024, t_rows)
    tn = 1024
    assert K_COL % tn + W_B <= tn and V_COL % tn + W_B <= tn
    rows_shape = jax.ShapeDtypeStruct((depth, t_rows * H_B, DH_B), F32)
    rows_spec = pl.BlockSpec((1, tm * H_B, DH_B), lambda i, j: (layer, i, 0))
    carried = [] if k_buf is None else [k_buf, v_buf]
    return pl.pallas_call(
        functools.partial(_inproj_kernel, tm=tm, tn=tn),
        grid=(t_rows // tm, H_COLS // tn),
        in_specs=[pl.BlockSpec((tm, D_MODEL), lambda i, j: (i, 0)),
                  pl.BlockSpec((D_MODEL, tn), lambda i, j: (0, j))]
                 + [pl.BlockSpec(memory_space=pl.ANY)] * len(carried),
        out_specs=[pl.BlockSpec((tm, tn), lambda i, j: (i, j)), rows_spec, rows_spec],
        out_shape=[jax.ShapeDtypeStruct((t_rows, H_COLS), F32), rows_shape, rows_shape],
        input_output_aliases={2: 1, 3: 2} if carried else {},
        compiler_params=_cparams(("parallel", "arbitrary")),
        name="in_proj",
    )(x2d, w, *carried)


def _delta_kernel(x_ref, qkv_ref, z_ref, wba_ref, conv_ref, prm_ref, s0_ref, db0_ref,
                  o_ref, s_out_ref, tail_ref,
                  s_scr, ext_scr, q_scr, k_scr, v_scr, bg_scr, o_scr, u_scr, wq_scr, qk_scr, kd_scr, gl_scr,
                  *, nb, tc, cp):
    c = CHUNK_A
    t = pl.program_id(1)

    @pl.when(t == 0)
    def _():
        s_scr[...] = s0_ref[...]
        ext_scr[:, 0:HALO, :] = db0_ref[...]

    w = conv_ref[...]
    for bb in range(nb):
        u = qkv_ref[bb]
        ext_scr[bb, HALO:HALO + tc, :] = u
        y = (ext_scr[bb, HALO - 3:HALO - 3 + tc, :] * w[0:1, :] + ext_scr[bb, HALO - 2:HALO - 2 + tc, :] * w[1:2, :]
             + ext_scr[bb, HALO - 1:HALO - 1 + tc, :] * w[2:3, :] + u * w[3:4, :])
        tail = ext_scr[bb, tc:tc + HALO, :]
        ext_scr[bb, 0:HALO, :] = tail
        tail_ref[bb] = tail
        y = _silu(y)

        ba = _dot(x_ref[bb].astype(BF16), wba_ref[...])
        beta = _sigmoid(ba)
        g = -jnp.exp(prm_ref[0:1, :]) * _softplus(ba + prm_ref[1:2, :])
        lane = lax.broadcasted_iota(jnp.int32, (tc, LANES), 1)
        bg = jnp.where(lane < H_A, beta, g)

        if cp > tc:
            q_scr[bb, tc:cp, :] = jnp.zeros((cp - tc, W_QK_A), F32)
            k_scr[bb, tc:cp, :] = jnp.zeros((cp - tc, W_QK_A), F32)
            v_scr[bb, tc:cp, :] = jnp.zeros((cp - tc, W_V_A), F32)
            bg_scr[bb, tc:cp, :] = jnp.zeros((cp - tc, LANES), F32)
        bg_scr[bb, 0:tc, :] = bg
        for h in range(H_A):
            qh = y[:, h * DK_A:(h + 1) * DK_A]
            kh = y[:, W_QK_A + h * DK_A:W_QK_A + (h + 1) * DK_A]
            qn = qh * lax.rsqrt(jnp.sum(qh * qh, axis=-1, keepdims=True) + RMS_EPS) * (DK_A ** -0.5)
            kn = kh * lax.rsqrt(jnp.sum(kh * kh, axis=-1, keepdims=True) + RMS_EPS)
            q_scr[bb, 0:tc, h * DK_A:(h + 1) * DK_A] = qn
            k_scr[bb, 0:tc, h * DK_A:(h + 1) * DK_A] = kn
        v_scr[bb, 0:tc, :] = y[:, 2 * W_QK_A:]

    ri = lax.broadcasted_iota(jnp.int32, (c, c), 0)
    ci = lax.broadcasted_iota(jnp.int32, (c, c), 1)
    tril = (ri >= ci).astype(BF16)
    triu_f = (ri <= ci).astype(F32)
    ones = jnp.ones((c, c), BF16)

    def prep_many(ins):
        qcs, kcs, vcs, betas, gs = zip(*ins)
        gbs = [jnp.broadcast_to(g_c, (c, LANES)) for g_c in gs]
        gcs = [_mm_sel(tril, gb) for gb in gbs]
        grows = [_mm_sel(ones, gb[:, :c] * triu_f) for gb in gbs]
        decs = [jnp.exp(jnp.where(ri >= ci, gc[:, :c] - gr, -jnp.inf)) for gc, gr in zip(gcs, grows)]
        kbs = [kc * b for kc, b in zip(kcs, betas)]
        kqs = [_dot_nt(jnp.concatenate([kb, qc], axis=0).astype(BF16), kc.astype(BF16))
               for kb, qc, kc in zip(kbs, qcs, kcs)]
        qks = [kq[c:] * dec for kq, dec in zip(kqs, decs)]
        egcs = [jnp.exp(gc) for gc in gcs]
        rhss = [jnp.concatenate([vc * b, kb * egc], axis=1) for vc, b, kb, egc in zip(vcs, betas, kbs, egcs)]
        ns = [-jnp.where(ri > ci, kq[:c] * dec, 0.0) for kq, dec in zip(kqs, decs)]
        yys = ns
        for _ in range(5):
            nbs = [n.astype(BF16) for n in ns]
            ns = [_dot(nb, nb) for nb in nbs]
            prods = [_dot(yy.astype(BF16), n.astype(BF16)) for yy, n in zip(yys, ns)]
            yys = [yy + n + p for yy, n, p in zip(yys, ns, prods)]
        sols = [rhs + _dot(yy.astype(BF16), rhs.astype(BF16)) for rhs, yy in zip(rhss, yys)]
        gls = [gc[c - 1:c, :] for gc in gcs]
        outs = []
        for sol, qc, kc, qk, egc, gc, gl in zip(sols, qcs, kcs, qks, egcs, gcs, gls):
            wq = jnp.concatenate([sol[:, DV_A:], qc * egc], axis=0).astype(BF16)
            k_dec = (kc * jnp.exp(gl - gc)).astype(BF16)
            outs.append((sol[:, :DV_A], wq, qk.astype(BF16), k_dec,
                         jnp.broadcast_to(jnp.exp(gl), (SUBLANES, LANES))))
        return outs

    n_chunks = cp // c
    problems = [(bb, ic, h) for bb in range(nb) for ic in range(n_chunks) for h in range(H_A)]
    for p0 in range(0, len(problems), PREP_GROUP):
        group = problems[p0:p0 + PREP_GROUP]
        ins = []
        for bb, ic, h in group:
            rows = slice(ic * c, (ic + 1) * c)
            ins.append((q_scr[bb, rows, h * DK_A:(h + 1) * DK_A], k_scr[bb, rows, h * DK_A:(h + 1) * DK_A],
                        v_scr[bb, rows, h * DV_A:(h + 1) * DV_A], bg_scr[bb, rows, h:h + 1],
                        bg_scr[bb, rows, H_A + h:H_A + h + 1]))
        for (bb, ic, h), (uu, wq, qk, k_dec, egl) in zip(group, prep_many(ins)):
            rows = slice(ic * c, (ic + 1) * c)
            u_scr[bb, rows, h * DV_A:(h + 1) * DV_A] = uu
            wq_scr[bb, 2 * ic * c:2 * (ic + 1) * c, h * DK_A:(h + 1) * DK_A] = wq
            qk_scr[bb, rows, h * LANES:h * LANES + c] = qk
            kd_scr[bb, rows, h * DK_A:(h + 1) * DK_A] = k_dec
            gl_scr[bb, ic * SUBLANES:(ic + 1) * SUBLANES, h * LANES:(h + 1) * LANES] = egl

    chains = [(bb, h) for bb in range(nb) for h in range(H_A)]
    ss = [s_scr[bb, h] for bb, h in chains]
    for ic in range(n_chunks):
        rows = slice(ic * c, (ic + 1) * c)
        us = [u_scr[bb, rows, h * DV_A:(h + 1) * DV_A] for bb, h in chains]
        wqs = [wq_scr[bb, 2 * ic * c:2 * (ic + 1) * c, h * DK_A:(h + 1) * DK_A] for bb, h in chains]
        qks = [qk_scr[bb, rows, h * LANES:h * LANES + c] for bb, h in chains]
        kds = [kd_scr[bb, rows, h * DK_A:(h + 1) * DK_A] for bb, h in chains]
        egls = [gl_scr[bb, ic * SUBLANES:ic * SUBLANES + 1, h * LANES:(h + 1) * LANES] for bb, h in chains]
        wss = [_dot(wq, s.astype(BF16)) for wq, s in zip(wqs, ss)]
        vns = [(uu - ws[:c]).astype(BF16) for uu, ws in zip(us, wss)]
        ocs = [ws[c:] + _dot(qk, vn) for ws, qk, vn in zip(wss, qks, vns)]
        ss = [s * egl + _dot_tn(kd, vn) for s, egl, kd, vn in zip(ss, egls, kds, vns)]
        for (bb, h), o_c in zip(chains, ocs):
            o_scr[bb, rows, h * DV_A:(h + 1) * DV_A] = o_c
    for (bb, h), s in zip(chains, ss):
        s_scr[bb, h] = s
        s_out_ref[bb, h] = s

    nw = prm_ref[2:3, :]
    for bb in range(nb):
        z = z_ref[bb]
        for h in range(H_A):
            oh = o_scr[bb, 0:tc, h * DV_A:(h + 1) * DV_A]
            zh = z[:, h * DV_A:(h + 1) * DV_A]
            oh = oh * lax.rsqrt(jnp.mean(oh * oh, axis=-1, keepdims=True) + RMS_EPS)
            o_ref[bb, :, h * DV_A:(h + 1) * DV_A] = oh * nw * _silu(zh)


def delta_mixer(x, h, wba, conv_a, prm, s0, db0):
    b, l, _ = x.shape
    tc = min(256, l)
    cp = max(tc, CHUNK_A)
    nt = l // tc
    nb = min(b, max(1, PREP_GROUP // (H_A * (cp // CHUNK_A))) if l < CHUNK_A else 2)
    assert b % nb == 0
    kern = functools.partial(_delta_kernel, nb=nb, tc=tc, cp=cp)
    return pl.pallas_call(
        kern,
        grid=(b // nb, nt),
        in_specs=[pl.BlockSpec((nb, tc, D_MODEL), lambda i, t: (i, t, 0)),
                  pl.BlockSpec((nb, tc, QKVA_W), lambda i, t: (i, t, QKVA_OFF // QKVA_W)),
                  pl.BlockSpec((nb, tc, Z_W), lambda i, t: (i, t, Z_OFF // Z_W)),
                  pl.BlockSpec((D_MODEL, LANES), lambda i, t: (0, 0)),
                  pl.BlockSpec((CONV_A, W_QKV_A), lambda i, t: (0, 0)),
                  pl.BlockSpec((SUBLANES, LANES), lambda i, t: (0, 0)),
                  pl.BlockSpec((nb, H_A, DK_A, DV_A), lambda i, t: (i, 0, 0, 0)),
                  pl.BlockSpec((nb, HALO, W_QKV_A), lambda i, t: (i, 0, 0))],
        out_specs=[pl.BlockSpec((nb, tc, W_V_A), lambda i, t: (i, t, 0)),
                   pl.BlockSpec((nb, H_A, DK_A, DV_A), lambda i, t: (i, 0, 0, 0)),
                   pl.BlockSpec((nb, HALO, W_QKV_A), lambda i, t: (i, 0, 0))],
        out_shape=[jax.ShapeDtypeStruct((b, l, W_V_A), F32),
                   jax.ShapeDtypeStruct((b, H_A, DK_A, DV_A), F32),
                   jax.ShapeDtypeStruct((b, HALO, W_QKV_A), F32)],
        scratch_shapes=[pltpu.VMEM((nb, H_A, DK_A, DV_A), F32),
                        pltpu.VMEM((nb, tc + HALO, W_QKV_A), F32),
                        pltpu.VMEM((nb, cp, W_QK_A), F32),
                        pltpu.VMEM((nb, cp, W_QK_A), F32),
                        pltpu.VMEM((nb, cp, W_V_A), F32),
                        pltpu.VMEM((nb, cp, LANES), F32),
                        pltpu.VMEM((nb, cp, W_V_A), F32),
                        pltpu.VMEM((nb, cp, W_V_A), F32),
                        pltpu.VMEM((nb, 2 * cp, W_QK_A), BF16),
                        pltpu.VMEM((nb, cp, H_A * LANES), BF16),
                        pltpu.VMEM((nb, cp, W_QK_A), BF16),
                        pltpu.VMEM((nb, cp // CHUNK_A * SUBLANES, H_A * LANES), F32)],
        compiler_params=_cparams(("parallel", "arbitrary")),
        name="delta_mixer",
    )(x, h, h, wba, conv_a, prm, s0, db0)


def _sb_group(tiles, lm, bias, r0, chained, first_key_cols=1):
    zs, cs, tots = _sb_scores([(t[0], t[1], t[3]) for t in tiles], lm, bias, first_key_cols)
    return _sb_apply(zs, cs, tots, [t[2] for t in tiles], [t[3] for t in tiles], r0, chained)


def _sb_scores(tiles, lm, bias, first_key_cols):
    zs = [_dot_nt(q, kb) + bias for q, kb, _ in tiles]
    cs, tots = _sb_cumulate(zs, [t[2] for t in tiles], lm, first_key_cols)
    return zs, cs, tots


def _sb_cumulate(zs, valids, lm, first_key_cols):
    sps = [jnp.maximum(z, 0.0) + jnp.log(1.0 + jnp.exp(_neg_abs(z))) for z in zs]
    sps = [sp if v is None else jnp.where(v, sp, 0.0) for sp, v in zip(sps, valids)]
    later = [_dot(sp.astype(BF16), lm) for sp in sps]
    tots = [lt[:, 0:1] + jnp.sum(sp[:, 0:first_key_cols], axis=-1, keepdims=True) for lt, sp in zip(later, sps)]
    return [sp + lt for sp, lt in zip(sps, later)], tots


def _sb_apply(zs, cs, tots, vbs, valids, r0, chained):
    rs, ws = [], []
    r = r0
    for i, (z, c, tot, valid) in enumerate(zip(zs, cs, tots, valids)):
        r_prev = r if chained else r0[i]
        a = jnp.exp(z - c - r_prev)
        if valid is not None:
            a = jnp.where(valid, a, 0.0)
        ws.append(a.astype(BF16))
        r = r_prev + tot
        rs.append(r)
    return [_dot(a, vb) for a, vb in zip(ws, vbs)], rs


def _attn_prompt_kernel(bias_ref, q_ref, k_ref, v_ref, lm_ref, o_ref, kbf, vbf, acc, rsum, zbuf, cbuf, tbuf,
                        *, tq, tk, ts):
    h = pl.program_id(1)
    qi = pl.program_id(2)
    nsub = tq // ts
    ndiag = tq // tk

    @pl.when(qi == 0)
    def _():
        kbf[...] = k_ref[0].astype(BF16)
        vbf[...] = v_ref[0].astype(BF16)

    q = (q_ref[0] * (DH_B ** -0.5)).astype(BF16)
    qs = [q[s * ts:(s + 1) * ts] for s in range(nsub)]
    bias = bias_ref[h]
    lm = lm_ref[...]
    row = lax.broadcasted_iota(jnp.int32, (ts, tk), 0)
    col = lax.broadcasted_iota(jnp.int32, (ts, tk), 1)
    started = set()

    def visit(k0, rel):
        kb = kbf[pl.ds(k0, tk), :]
        vb = vbf[pl.ds(k0, tk), :]
        subs, tiles, r0 = [], [], []
        for s in range(nsub):
            off = tk if rel is None else s * ts - rel
            if off + ts - 1 <= 0:
                continue
            valid = None if off >= tk else col < row + off
            subs.append(s)
            tiles.append((qs[s], kb, vb, valid))
            r0.append(rsum[s * ts:(s + 1) * ts, :] if s in started else jnp.zeros((ts, 1), F32))
        old = [acc[s * ts:(s + 1) * ts, :] if s in started else None for s in subs]
        pvs, rs = _sb_group(tiles, lm, bias, r0, False)
        for s, o, pv, r in zip(subs, old, pvs, rs):
            acc[s * ts:(s + 1) * ts, :] = pv if o is None else o + pv
            rsum[s * ts:(s + 1) * ts, :] = r
            started.add(s)

    q0 = qi * tq
    for d in reversed(range(ndiag)):
        visit(pl.multiple_of(q0 + d * tk, tk), d * tk)

    n_past = qi * ndiag

    def past_block(i):
        return pl.multiple_of(jnp.maximum(q0 - (i + 1) * tk, 0), tk)

    rows = [slice(s * ts, (s + 1) * ts) for s in range(nsub)]

    def qk(i):
        kb = kbf[pl.ds(past_block(i), tk), :]
        return [_dot_nt(qs[s], kb) + bias for s in range(nsub)]

    def finish_scores(zs, slot):
        cs, tots = _sb_cumulate(zs, [None] * nsub, lm, 1)
        for r, z, c, t in zip(rows, zs, cs, tots):
            zbuf[slot, r, :] = z
            cbuf[slot, r, :] = c
            tbuf[slot, r, :] = t

    def apply(i, slot):
        vb = vbf[pl.ds(past_block(i), tk), :]
        pvs, rs = _sb_apply([zbuf[slot, r, :] for r in rows], [cbuf[slot, r, :] for r in rows],
                            [tbuf[slot, r, :] for r in rows], [vb] * nsub, [None] * nsub,
                            [rsum[r, :] for r in rows], False)
        for r, pv, rn in zip(rows, pvs, rs):
            acc[r, :] += pv
            rsum[r, :] = rn

    @pl.when(n_past > 0)
    def _():
        finish_scores(qk(0), 0)

    def body(j, carry):
        for slot in range(2):
            i = 2 * j + slot
            z_next = qk(i + 1)
            apply(i, slot)
            finish_scores(z_next, 1 - slot)
        return carry

    lax.fori_loop(0, n_past // 2, body, 0)
    o_ref[0] = acc[...]


def _later_ones(n):
    r = lax.broadcasted_iota(jnp.int32, (n, n), 0)
    c = lax.broadcasted_iota(jnp.int32, (n, n), 1)
    return (r > c).astype(BF16)


def attn_prompt(h, sb_bias):
    b, l, _ = h.shape
    tq = min(1024, l)
    tk = min(256, tq)
    ts = min(128, tq)
    assert l == tq or (tq // tk) % 2 == 0
    qb = QKVB_OFF // DH_B
    kern = functools.partial(_attn_prompt_kernel, tq=tq, tk=tk, ts=ts)
    grid_spec = pltpu.PrefetchScalarGridSpec(
        num_scalar_prefetch=0,
        grid=(b, H_B, l // tq),
        in_specs=[pl.BlockSpec(memory_space=pltpu.SMEM),
                  pl.BlockSpec((1, tq, DH_B), lambda i, hh, j: (i, j, qb + hh)),
                  pl.BlockSpec((1, l, DH_B), lambda i, hh, j: (i, 0, qb + H_B + hh)),
                  pl.BlockSpec((1, l, DH_B), lambda i, hh, j: (i, 0, qb + 2 * H_B + hh)),
                  pl.BlockSpec((tk, tk), lambda i, hh, j: (0, 0))],
        out_specs=pl.BlockSpec((1, tq, DH_B), lambda i, hh, j: (i, j, hh)),
        scratch_shapes=[pltpu.VMEM((l, DH_B), BF16),
                        pltpu.VMEM((l, DH_B), BF16),
                        pltpu.VMEM((tq, DH_B), F32),
                        pltpu.VMEM((tq, 1), F32),
                        pltpu.VMEM((2, tq, tk), F32),
                        pltpu.VMEM((2, tq, tk), F32),
                        pltpu.VMEM((2, tq, 1), F32)])
    return pl.pallas_call(
        kern,
        grid_spec=grid_spec,
        out_shape=jax.ShapeDtypeStruct((b, l, W_B), F32),
        compiler_params=_cparams(("parallel", "parallel", "arbitrary")),
        name="attn_prompt",
    )(sb_bias, h, h, h, _later_ones(tk))


def _attn_sample_kernel(pt_ref, bias_ref, qkv_ref, *rest, seq, pages_per_step, page_group, n_steps):
    g_pages = pages_per_step
    k_refs = rest[:g_pages]
    v_refs = rest[g_pages:2 * g_pages]
    lm_ref = rest[2 * g_pages]
    lx_ref = rest[2 * g_pages + 1]
    o_ref = rest[2 * g_pages + 2]
    qall, acc, rsum = rest[2 * g_pages + 3:]
    s = pl.program_id(1)
    rows = H_B * seq
    pcols = PAGE_SIZE * H_B
    rid = lax.broadcasted_iota(jnp.int32, (rows, 1), 0)
    bias = jnp.zeros((rows, 1), F32)
    for hh in range(H_B):
        bias = jnp.where((rid >= hh * seq) & (rid < (hh + 1) * seq), bias_ref[hh], bias)

    @pl.when(s == 0)
    def _():
        qkv = qkv_ref[0]
        q = qkv[:, 0:W_B] * (DH_B ** -0.5)
        qall[...] = jnp.concatenate([q[:, hh * DH_B:(hh + 1) * DH_B] for hh in range(H_B)], axis=0).astype(BF16)
        lane = lax.broadcasted_iota(jnp.int32, (seq, W_B), 1)
        parts = [jnp.where((lane >= hh * DH_B) & (lane < (hh + 1) * DH_B), q, 0.0) for hh in range(H_B)]
        qbd = jnp.concatenate(parts, axis=0).astype(BF16)
        zrows = jnp.zeros((PAGE_SIZE - seq, W_B), F32)
        k_own = jnp.concatenate([qkv[:, W_B:2 * W_B], zrows], axis=0).astype(BF16)
        v_own = jnp.concatenate([qkv[:, 2 * W_B:3 * W_B], zrows], axis=0).astype(BF16)
        key = lax.broadcasted_iota(jnp.int32, (rows, PAGE_SIZE), 1)
        qpos = lax.broadcasted_iota(jnp.int32, (rows, PAGE_SIZE), 0) % seq
        pvs, rs = _sb_group([(qbd, k_own, v_own, key < qpos)], lm_ref[...], bias, jnp.zeros((rows, 1), F32), True)
        acc[...] = jnp.concatenate(
            [pvs[0][hh * seq:(hh + 1) * seq, hh * DH_B:(hh + 1) * DH_B] for hh in range(H_B)], axis=0)
        rsum[...] = rs[0]

    own = (lax.broadcasted_iota(jnp.int32, (rows, pcols), 1) % H_B
           == lax.broadcasted_iota(jnp.int32, (rows, pcols), 0) // seq)
    qa = qall[...]
    lx = lx_ref[...]
    r = rsum[...]
    total = acc[...]
    for g0 in reversed(range(0, g_pages, page_group)):
        tiles = [(qa, k_refs[gi][0, 0].astype(BF16), v_refs[gi][0, 0].astype(BF16), own)
                 for gi in reversed(range(g0, g0 + page_group))]
        pvs, rs = _sb_group(tiles, lx, bias, r, True, H_B)
        r = rs[-1]
        for pv in pvs:
            total = total + pv
    acc[...] = total
    rsum[...] = r

    @pl.when(s == n_steps - 1)
    def _():
        a = acc[...]
        o_ref[0] = jnp.concatenate([a[hh * seq:(hh + 1) * seq, :] for hh in range(H_B)], axis=1)


def attn_sample(h, cache_k, cache_v, page_table, sb_bias, layer):
    b, seq, _ = h.shape
    depth, n_pool = cache_k.shape[:2]
    n_pages = page_table.shape[0] // b
    g_pages = min(16, n_pages)
    n_steps = n_pages // g_pages
    rows = H_B * seq
    pcols = PAGE_SIZE * H_B
    ck = cache_k.reshape(depth, n_pool, pcols, DH_B)
    cv = cache_v.reshape(depth, n_pool, pcols, DH_B)
    kidx = lax.broadcasted_iota(jnp.int32, (pcols, pcols), 0) // H_B
    lexp = (kidx > kidx.T).astype(BF16)

    def page_map(gi):
        def index_map(i, s, pt):
            return (layer, pt[i * n_pages + (n_steps - 1 - s) * g_pages + gi], 0, 0)
        return index_map

    page_specs = [pl.BlockSpec((1, 1, pcols, DH_B), page_map(gi)) for gi in range(g_pages)]
    kern = functools.partial(_attn_sample_kernel, seq=seq, pages_per_step=g_pages, page_group=min(4, g_pages),
                             n_steps=n_steps)
    grid_spec = pltpu.PrefetchScalarGridSpec(
        num_scalar_prefetch=1,
        grid=(b, n_steps),
        in_specs=([pl.BlockSpec(memory_space=pltpu.SMEM),
                   pl.BlockSpec((1, seq, QKVB_W), lambda i, s, pt: (i, 0, QKVB_OFF // QKVB_W))]
                  + page_specs + page_specs
                  + [pl.BlockSpec((PAGE_SIZE, PAGE_SIZE), lambda i, s, pt: (0, 0)),
                     pl.BlockSpec((pcols, pcols), lambda i, s, pt: (0, 0))]),
        out_specs=pl.BlockSpec((1, seq, W_B), lambda i, s, pt: (i, 0, 0)),
        scratch_shapes=[pltpu.VMEM((rows, DH_B), BF16),
                        pltpu.VMEM((rows, DH_B), F32),
                        pltpu.VMEM((rows, 1), F32)])
    return pl.pallas_call(
        kern,
        grid_spec=grid_spec,
        out_shape=jax.ShapeDtypeStruct((b, seq, W_B), F32),
        compiler_params=_cparams(("parallel", "arbitrary")),
        name="attn_sample",
    )(page_table, sb_bias, h, *([ck] * g_pages), *([cv] * g_pages), _later_ones(PAGE_SIZE), lexp)


def _merge_kernel(x_ref, g_ref, c_ref, oa_ref, ob_ref, wpa_ref, wpb_ref, wpc_ref, wo_ref, conv_ref, ln_ref, sb0_ref,
                  o_ref, tail_ref, ext_scr, *, tm):
    t = pl.program_id(1)

    @pl.when(t == 0)
    def _():
        ext_scr[0:HALO, :] = sb0_ref[0]

    cc = c_ref[0]
    u = cc[:, 2 * W_C:] * cc[:, :W_C]
    ext_scr[HALO:HALO + tm, :] = u
    w = conv_ref[...]
    y = ext_scr[HALO - 2:HALO - 2 + tm, :] * w[0:1, :] + ext_scr[HALO - 1:HALO - 1 + tm, :] * w[1:2, :] + u * w[2:3, :]
    tail = ext_scr[tm:tm + HALO, :]
    ext_scr[0:HALO, :] = tail
    tail_ref[0] = tail
    o_c = cc[:, W_C:2 * W_C] * y

    g = g_ref[0]
    merged = (_sigmoid(g[:, :D_MODEL]) * _dot(oa_ref[0].astype(BF16), wpa_ref[...])
              + _sigmoid(g[:, D_MODEL:2 * D_MODEL]) * _dot(ob_ref[0].astype(BF16), wpb_ref[...])
              + _sigmoid(g[:, 2 * D_MODEL:]) * _dot(o_c.astype(BF16), wpc_ref[...]))
    r = DEEPNORM_ALPHA * x_ref[0] + _dot(merged.astype(BF16), wo_ref[...])
    o_ref[0] = _layer_norm(r, ln_ref[0:1, :], ln_ref[1:2, :])


def merge(x, h, o_a, o_b, wpa, wpb, wpc, wo, conv_c, ln, sb0):
    b, l, _ = x.shape
    tm = min(256, l)
    kern = functools.partial(_merge_kernel, tm=tm)
    const = lambda i, t: (0, 0)
    return pl.pallas_call(
        kern,
        grid=(b, l // tm),
        in_specs=[pl.BlockSpec((1, tm, D_MODEL), lambda i, t: (i, t, 0)),
                  pl.BlockSpec((1, tm, G_W), lambda i, t: (i, t, G_OFF // G_W)),
                  pl.BlockSpec((1, tm, C_W), lambda i, t: (i, t, C_OFF // C_W)),
                  pl.BlockSpec((1, tm, W_V_A), lambda i, t: (i, t, 0)),
                  pl.BlockSpec((1, tm, W_B), lambda i, t: (i, t, 0)),
                  pl.BlockSpec((W_V_A, D_MODEL), const),
                  pl.BlockSpec((W_B, D_MODEL), const),
                  pl.BlockSpec((W_C, D_MODEL), const),
                  pl.BlockSpec((D_MODEL, D_MODEL), const),
                  pl.BlockSpec((CONV_C, W_C), const),
                  pl.BlockSpec((2, D_MODEL), const),
                  pl.BlockSpec((1, HALO, W_C), lambda i, t: (i, 0, 0))],
        out_specs=[pl.BlockSpec((1, tm, D_MODEL), lambda i, t: (i, t, 0)),
                   pl.BlockSpec((1, HALO, W_C), lambda i, t: (i, 0, 0))],
        out_shape=[jax.ShapeDtypeStruct((b, l, D_MODEL), F32),
                   jax.ShapeDtypeStruct((b, HALO, W_C), F32)],
        scratch_shapes=[pltpu.VMEM((tm + HALO, W_C), F32)],
        compiler_params=_cparams(("parallel", "arbitrary")),
        name="merge",
    )(x, h, h, o_a, o_b, wpa, wpb, wpc, wo, conv_c, ln, sb0)


def _moe_kernel(x_ref, wr_ref, br_ref, wgu_ref, wdn_ref, ln_ref, o_ref, xb, gates, acc, *, tm):
    e = pl.program_id(1)
    lane = lax.broadcasted_iota(jnp.int32, (tm, LANES), 1)

    @pl.when(e == 0)
    def _():
        x = x_ref[...]
        xh, xl = _split2(x)
        wh, wl = _split2(wr_ref[...])
        logits = _dot(xh, wh) + _dot(xh, wl) + _dot(xl, wh) + br_ref[...]
        lanef = lane.astype(F32)
        big = float(LANES)
        is_g = lane < N_GROUPS
        gl = jnp.where(is_g, logits, -jnp.inf)
        gmax = jnp.max(gl, axis=-1, keepdims=True)
        gsel = jnp.min(jnp.where(gl == gmax, lanef, big), axis=-1, keepdims=True)
        pg_sel = 1.0 / jnp.sum(jnp.where(is_g, jnp.exp(gl - gmax), 0.0), axis=-1, keepdims=True)
        lo = N_GROUPS + gsel * EXPERTS_PER_GROUP
        ev = jnp.where((lanef >= lo) & (lanef < lo + EXPERTS_PER_GROUP), logits, -jnp.inf)
        v1 = jnp.max(ev, axis=-1, keepdims=True)
        i1 = jnp.min(jnp.where(ev == v1, lanef, big), axis=-1, keepdims=True)
        ev2 = jnp.where(lanef == i1, -jnp.inf, ev)
        v2 = jnp.max(ev2, axis=-1, keepdims=True)
        i2 = jnp.min(jnp.where(ev2 == v2, lanef, big), axis=-1, keepdims=True)
        e2 = jnp.exp(v2 - v1)
        den = 1.0 + e2
        gates[...] = jnp.where(lanef == i1, pg_sel / den, jnp.where(lanef == i2, pg_sel * e2 / den, 0.0))
        xb[...] = xh
        acc[...] = jnp.zeros((tm, D_MODEL), F32)

    ge = jnp.sum(jnp.where(lane == N_GROUPS + e, gates[...], 0.0), axis=-1, keepdims=True)
    hh = _dot(xb[...], wgu_ref[0])
    act = _silu(hh[:, :D_EXPERT]) * hh[:, D_EXPERT:] * ge
    acc[...] += _dot(act.astype(BF16), wdn_ref[0])

    @pl.when(e == N_EXPERTS - 1)
    def _():
        r = DEEPNORM_ALPHA * x_ref[...] + acc[...]
        o_ref[...] = _layer_norm(r, ln_ref[0:1, :], ln_ref[1:2, :])


def moe(x2d, wr, br, wgu, wdn, ln):
    t_rows = x2d.shape[0]
    tm = min(1024, t_rows)
    kern = functools.partial(_moe_kernel, tm=tm)
    return pl.pallas_call(
        kern,
        grid=(t_rows // tm, N_EXPERTS),
        in_specs=[pl.BlockSpec((tm, D_MODEL), lambda i, e: (i, 0)),
                  pl.BlockSpec((D_MODEL, LANES), lambda i, e: (0, 0)),
                  pl.BlockSpec((1, LANES), lambda i, e: (0, 0)),
                  pl.BlockSpec((1, D_MODEL, 2 * D_EXPERT), lambda i, e: (e, 0, 0)),
                  pl.BlockSpec((1, D_EXPERT, D_MODEL), lambda i, e: (e, 0, 0)),
                  pl.BlockSpec((2, D_MODEL), lambda i, e: (0, 0))],
        out_specs=pl.BlockSpec((tm, D_MODEL), lambda i, e: (i, 0)),
        out_shape=jax.ShapeDtypeStruct((t_rows, D_MODEL), F32),
        scratch_shapes=[pltpu.VMEM((tm, D_MODEL), BF16),
                        pltpu.VMEM((tm, LANES), F32),
                        pltpu.VMEM((tm, D_MODEL), F32)],
        compiler_params=_cparams(("parallel", "arbitrary")),
        name="moe",
    )(x2d, wr, br, wgu, wdn, ln)


def _prep_layer(l, w_in, conv_a, a_log, dt_bias, norm_a, conv_c, w_pa, w_pb, w_pc, w_o, ln_g, ln_b,
                w_rg, b_rg, w_re, b_re, w_gu, w_down):
    wi = w_in[l]
    o_z = W_QKV_A
    o_ba = o_z + W_V_A
    o_qb = o_ba + 2 * H_A
    o_c = o_qb + 3 * W_B
    o_g = o_c + 3 * W_C
    w_main = jnp.concatenate([wi[:, o_g:], wi[:, :W_QKV_A], wi[:, o_c:o_g], wi[:, o_qb:o_c], wi[:, o_z:o_ba]],
                             axis=1).astype(BF16)
    w_ba = jnp.pad(wi[:, o_ba:o_qb], ((0, 0), (0, LANES - 2 * H_A))).astype(BF16)
    prm = jnp.zeros((SUBLANES, LANES), F32)
    prm = prm.at[0, H_A:2 * H_A].set(a_log[l]).at[1, H_A:2 * H_A].set(dt_bias[l]).at[2, :DV_A].set(norm_a[l])
    w_r = jnp.pad(jnp.concatenate([w_rg[l], w_re[l]], axis=1), ((0, 0), (0, LANES - N_GROUPS - N_EXPERTS)))
    b_r = jnp.pad(jnp.concatenate([b_rg[l], b_re[l]]), (0, LANES - N_GROUPS - N_EXPERTS)).reshape(1, LANES)
    return dict(
        w_main=w_main, w_ba=w_ba, conv_a=conv_a[l], prm=prm, conv_c=conv_c[l],
        w_pa=w_pa[l].astype(BF16), w_pb=w_pb[l].astype(BF16), w_pc=w_pc[l].astype(BF16), w_o=w_o[l].astype(BF16),
        ln0=jnp.stack([ln_g[l, 0], ln_b[l, 0]]), ln1=jnp.stack([ln_g[l, 1], ln_b[l, 1]]),
        w_r=w_r, b_r=b_r, w_gu=w_gu[l].astype(BF16), w_down=w_down[l].astype(BF16))


def _pad_tail(buf):
    return jnp.pad(buf, ((0, 0), (HALO - buf.shape[1], 0), (0, 0)))


def _layer(x, p, s0, db0, sb0, attn_fn, layer, depth, k_buf, v_buf):
    b, l, _ = x.shape
    h, k_buf, v_buf = in_proj(x.reshape(b * l, D_MODEL), p["w_main"], layer, depth, k_buf, v_buf)
    h = h.reshape(b, l, H_COLS)
    o_a, s_new, dtail = delta_mixer(x, h, p["w_ba"], p["conv_a"], p["prm"], s0, _pad_tail(db0))
    o_b = attn_fn(h)
    x1, stail = merge(x, h, o_a, o_b, p["w_pa"], p["w_pb"], p["w_pc"], p["w_o"], p["conv_c"], p["ln0"], _pad_tail(sb0))
    x2 = moe(x1.reshape(b * l, D_MODEL), p["w_r"], p["b_r"], p["w_gu"], p["w_down"], p["ln1"]).reshape(b, l, D_MODEL)
    return (x2, s_new, dtail[:, HALO - (CONV_A - 1):], stail[:, HALO - (CONV_C - 1):]), k_buf, v_buf


def kernel(x_prompt, x_sample, cache_k, cache_v, state_delta, state_dconv, state_sconv, page_table, w_in, conv_a, a_log, dt_bias, norm_a, sb_bias, conv_c, w_pa, w_pb, w_pc, w_o, ln_g, ln_b, w_rg, b_rg, w_re, b_re, w_gu, w_down):
    bp, lp, _ = x_prompt.shape
    bs, ls, _ = x_sample.shape
    depth = w_in.shape[0]
    pt = page_table.reshape(-1).astype(jnp.int32)
    xp, xs = x_prompt, x_sample
    outs_p = [[] for _ in range(3)]
    outs_s = [[] for _ in range(3)]
    pk = pv = sk = sv = None
    for l in range(depth):
        p = _prep_layer(l, w_in, conv_a, a_log, dt_bias, norm_a, conv_c, w_pa, w_pb, w_pc, w_o, ln_g, ln_b,
                        w_rg, b_rg, w_re, b_re, w_gu, w_down)
        bias = sb_bias[l]
        res, pk, pv = _layer(xp, p,
                             jnp.zeros((bp, H_A, DK_A, DV_A), F32),
                             jnp.zeros((bp, CONV_A - 1, W_QKV_A), F32),
                             jnp.zeros((bp, CONV_C - 1, W_C), F32),
                             lambda h: attn_prompt(h, bias), l, depth, pk, pv)
        xp = res[0]
        for acc_list, r in zip(outs_p, res[1:]):
            acc_list.append(r)
        res, sk, sv = _layer(xs, p, state_delta[l], state_dconv[l], state_sconv[l],
                             lambda h: attn_sample(h, cache_k, cache_v, pt, bias, l), l, depth, sk, sv)
        xs = res[0]
        for acc_list, r in zip(outs_s, res[1:]):
            acc_list.append(r)
    return (xp, xs, *[jnp.stack(o) for o in outs_p],
            pk.reshape(depth, bp, lp, H_B, DH_B), pv.reshape(depth, bp, lp, H_B, DH_B),
            *[jnp.stack(o) for o in outs_s],
            sk.reshape(depth, bs, ls, H_B, DH_B), sv.reshape(depth, bs, ls, H_B, DH_B))
```

```python
import functools

import jax
import jax.numpy as jnp
from jax import lax
from jax.experimental import pallas as pl
from jax.experimental.pallas import tpu as pltpu

F32 = jnp.float32
BF16 = jnp.bfloat16

D_MODEL = 1024
DEPTH = 4
H_A = 4
DK_A = 128
DV_A = 128
CONV_A = 4
CHUNK_A = 64
H_B = 4
DH_B = 128
W_C = 512
CONV_C = 3
N_GROUPS = 4
EXPERTS_PER_GROUP = 4
N_EXPERTS = N_GROUPS * EXPERTS_PER_GROUP
D_EXPERT = 256
PAGE_SIZE = 128

W_QK_A = H_A * DK_A
W_V_A = H_A * DV_A
W_QKV_A = 2 * W_QK_A + W_V_A
W_B = H_B * DH_B

DEEPNORM_ALPHA = (2.0 * DEPTH) ** 0.25
LN_EPS = 1e-5
RMS_EPS = 1e-6

G_OFF, G_W = 0, 3 * D_MODEL
QKVA_OFF, QKVA_W = 3072, W_QKV_A
C_OFF, C_W = 4608, 3 * W_C
QKVB_OFF, QKVB_W = 6144, 3 * W_B
Z_OFF, Z_W = 7680, W_V_A
H_COLS = 8192
LANES = 128
SUBLANES = 8
HALO = SUBLANES
PREP_GROUP = 16

VMEM_LIMIT = 56 * 1024 * 1024


def _cparams(sem):
    return pltpu.CompilerParams(dimension_semantics=sem, vmem_limit_bytes=VMEM_LIMIT)


def _dot(a, b):
    return jnp.dot(a, b, preferred_element_type=F32)


def _dot_nt(a, b):
    return lax.dot_general(a, b, (((1,), (1,)), ((), ())), preferred_element_type=F32)


def _dot_tn(a, b):
    return lax.dot_general(a, b, (((0,), (0,)), ((), ())), preferred_element_type=F32)


def _split2(x):
    hi = x.astype(BF16)
    lo = (x - hi.astype(F32)).astype(BF16)
    return hi, lo


def _split3(x):
    hi = x.astype(BF16)
    r = x - hi.astype(F32)
    mid = r.astype(BF16)
    lo = (r - mid.astype(F32)).astype(BF16)
    return hi, mid, lo


def _mm_sel(sel, x):
    hi, mid, lo = _split3(x)
    return _dot(sel, hi) + _dot(sel, mid) + _dot(sel, lo)


def _neg_abs(x):
    bits = lax.bitcast_convert_type(x, jnp.uint32) | jnp.uint32(0x80000000)
    return lax.bitcast_convert_type(bits, F32)


def _softplus(x):
    return jnp.maximum(x, 0.0) + jnp.log1p(jnp.exp(-jnp.abs(x)))


def _sigmoid(x):
    return 1.0 / (1.0 + jnp.exp(-x))


def _silu(x):
    return x * _sigmoid(x)


def _layer_norm(r, g, b):
    mu = jnp.mean(r, axis=-1, keepdims=True)
    d = r - mu
    var = jnp.mean(d * d, axis=-1, keepdims=True)
    return d * lax.rsqrt(var + LN_EPS) * g + b


K_COL = QKVB_OFF + W_B
V_COL = QKVB_OFF + 2 * W_B


def _inproj_kernel(x_ref, w_ref, *refs, tm, tn):
    o_ref, k_ref, v_ref = refs[-3:]
    j = pl.program_id(1)
    o_ref[...] = _dot(x_ref[...].astype(BF16), w_ref[...])

    def rows_out(ref, col):
        @pl.when(j == col // tn)
        def _():
            for hh in range(H_B):
                c0 = col % tn + hh * DH_B
                ref[0, pl.ds(hh, tm, stride=H_B), :] = o_ref[:, c0:c0 + DH_B]

    rows_out(k_ref, K_COL)
    rows_out(v_ref, V_COL)


def in_proj(x2d, w, layer, depth, k_buf, v_buf):
    t_rows = x2d.shape[0]
    tm = min(1024, t_rows)
    tn = 2048
    assert K_COL % tn + W_B <= tn and V_COL % tn + W_B <= tn
    rows_shape = jax.ShapeDtypeStruct((depth, t_rows * H_B, DH_B), F32)
    rows_spec = pl.BlockSpec((1, tm * H_B, DH_B), lambda i, j: (layer, i, 0))
    carried = [] if k_buf is None else [k_buf, v_buf]
    return pl.pallas_call(
        functools.partial(_inproj_kernel, tm=tm, tn=tn),
        grid=(t_rows // tm, H_COLS // tn),
        in_specs=[pl.BlockSpec((tm, D_MODEL), lambda i, j: (i, 0)),
                  pl.BlockSpec((None, D_MODEL, tn), lambda i, j: (layer, 0, j))]
                 + [pl.BlockSpec(memory_space=pl.ANY)] * len(carried),
        out_specs=[pl.BlockSpec((tm, tn), lambda i, j: (i, j)), rows_spec, rows_spec],
        out_shape=[jax.ShapeDtypeStruct((t_rows, H_COLS), F32), rows_shape, rows_shape],
        input_output_aliases={2: 1, 3: 2} if carried else {},
        compiler_params=_cparams(("parallel", "arbitrary")),
        name="in_proj",
    )(x2d, w, *carried)


def _delta_kernel(x_ref, qkv_ref, z_ref, wba_ref, conv_ref, prm_ref, s0_ref, db0_ref,
                  o_ref, s_out_ref, tail_ref,
                  s_scr, ext_scr, q_scr, k_scr, v_scr, bg_scr, o_scr, u_scr, wq_scr, qk_scr, kd_scr, gl_scr,
                  *, nb, tc, cp):
    c = CHUNK_A
    t = pl.program_id(1)

    @pl.when(t == 0)
    def _():
        s_scr[...] = s0_ref[...]
        ext_scr[:, 0:HALO, :] = db0_ref[...]

    w = conv_ref[...]
    for bb in range(nb):
        u = qkv_ref[bb]
        ext_scr[bb, HALO:HALO + tc, :] = u
        y = (ext_scr[bb, HALO - 3:HALO - 3 + tc, :] * w[0:1, :] + ext_scr[bb, HALO - 2:HALO - 2 + tc, :] * w[1:2, :]
             + ext_scr[bb, HALO - 1:HALO - 1 + tc, :] * w[2:3, :] + u * w[3:4, :])
        tail = ext_scr[bb, tc:tc + HALO, :]
        ext_scr[bb, 0:HALO, :] = tail
        tail_ref[bb] = tail
        y = _silu(y)

        ba = _dot(x_ref[bb].astype(BF16), wba_ref[...])
        beta = _sigmoid(ba)
        g = -jnp.exp(prm_ref[0:1, :]) * _softplus(ba + prm_ref[1:2, :])
        lane = lax.broadcasted_iota(jnp.int32, (tc, LANES), 1)
        bg = jnp.where(lane < H_A, beta, g)

        if cp > tc:
            q_scr[bb, tc:cp, :] = jnp.zeros((cp - tc, W_QK_A), F32)
            k_scr[bb, tc:cp, :] = jnp.zeros((cp - tc, W_QK_A), F32)
            v_scr[bb, tc:cp, :] = jnp.zeros((cp - tc, W_V_A), F32)
            bg_scr[bb, tc:cp, :] = jnp.zeros((cp - tc, LANES), F32)
        bg_scr[bb, 0:tc, :] = bg
        for h in range(H_A):
            qh = y[:, h * DK_A:(h + 1) * DK_A]
            kh = y[:, W_QK_A + h * DK_A:W_QK_A + (h + 1) * DK_A]
            qn = qh * lax.rsqrt(jnp.sum(qh * qh, axis=-1, keepdims=True) + RMS_EPS) * (DK_A ** -0.5)
            kn = kh * lax.rsqrt(jnp.sum(kh * kh, axis=-1, keepdims=True) + RMS_EPS)
            q_scr[bb, 0:tc, h * DK_A:(h + 1) * DK_A] = qn
            k_scr[bb, 0:tc, h * DK_A:(h + 1) * DK_A] = kn
        v_scr[bb, 0:tc, :] = y[:, 2 * W_QK_A:]

    ri = lax.broadcasted_iota(jnp.int32, (c, c), 0)
    ci = lax.broadcasted_iota(jnp.int32, (c, c), 1)
    tril = (ri >= ci).astype(BF16)
    triu_f = (ri <= ci).astype(F32)
    ones = jnp.ones((c, c), BF16)

    def prep_many(ins):
        qcs, kcs, vcs, betas, gs = zip(*ins)
        gbs = [jnp.broadcast_to(g_c, (c, LANES)) for g_c in gs]
        gcs = [_mm_sel(tril, gb) for gb in gbs]
        grows = [_mm_sel(ones, gb[:, :c] * triu_f) for gb in gbs]
        decs = [jnp.exp(jnp.where(ri >= ci, gc[:, :c] - gr, -jnp.inf)) for gc, gr in zip(gcs, grows)]
        kbs = [kc * b for kc, b in zip(kcs, betas)]
        kqs = [_dot_nt(jnp.concatenate([kb, qc], axis=0).astype(BF16), kc.astype(BF16))
               for kb, qc, kc in zip(kbs, qcs, kcs)]
        qks = [kq[c:] * dec for kq, dec in zip(kqs, decs)]
        egcs = [jnp.exp(gc) for gc in gcs]
        rhss = [jnp.concatenate([vc * b, kb * egc], axis=1) for vc, b, kb, egc in zip(vcs, betas, kbs, egcs)]
        ns = [-jnp.where(ri > ci, kq[:c] * dec, 0.0) for kq, dec in zip(kqs, decs)]
        yys = ns
        for _ in range(5):
            nbs = [n.astype(BF16) for n in ns]
            ns = [_dot(nb, nb) for nb in nbs]
            prods = [_dot(yy.astype(BF16), n.astype(BF16)) for yy, n in zip(yys, ns)]
            yys = [yy + n + p for yy, n, p in zip(yys, ns, prods)]
        sols = [rhs + _dot(yy.astype(BF16), rhs.astype(BF16)) for rhs, yy in zip(rhss, yys)]
        gls = [gc[c - 1:c, :] for gc in gcs]
        outs = []
        for sol, qc, kc, qk, egc, gc, gl in zip(sols, qcs, kcs, qks, egcs, gcs, gls):
            wq = jnp.concatenate([sol[:, DV_A:], qc * egc], axis=0).astype(BF16)
            k_dec = (kc * jnp.exp(gl - gc)).astype(BF16)
            outs.append((sol[:, :DV_A], wq, qk.astype(BF16), k_dec,
                         jnp.broadcast_to(jnp.exp(gl), (SUBLANES, LANES))))
        return outs

    n_chunks = cp // c
    problems = [(bb, ic, h) for bb in range(nb) for ic in range(n_chunks) for h in range(H_A)]
    for p0 in range(0, len(problems), PREP_GROUP):
        group = problems[p0:p0 + PREP_GROUP]
        ins = []
        for bb, ic, h in group:
            rows = slice(ic * c, (ic + 1) * c)
            ins.append((q_scr[bb, rows, h * DK_A:(h + 1) * DK_A], k_scr[bb, rows, h * DK_A:(h + 1) * DK_A],
                        v_scr[bb, rows, h * DV_A:(h + 1) * DV_A], bg_scr[bb, rows, h:h + 1],
                        bg_scr[bb, rows, H_A + h:H_A + h + 1]))
        for (bb, ic, h), (uu, wq, qk, k_dec, egl) in zip(group, prep_many(ins)):
            rows = slice(ic * c, (ic + 1) * c)
            u_scr[bb, rows, h * DV_A:(h + 1) * DV_A] = uu
            wq_scr[bb, 2 * ic * c:2 * (ic + 1) * c, h * DK_A:(h + 1) * DK_A] = wq
            qk_scr[bb, rows, h * LANES:h * LANES + c] = qk
            kd_scr[bb, rows, h * DK_A:(h + 1) * DK_A] = k_dec
            gl_scr[bb, ic * SUBLANES:(ic + 1) * SUBLANES, h * LANES:(h + 1) * LANES] = egl

    chains = [(bb, h) for bb in range(nb) for h in range(H_A)]
    ss = [s_scr[bb, h] for bb, h in chains]
    for ic in range(n_chunks):
        rows = slice(ic * c, (ic + 1) * c)
        us = [u_scr[bb, rows, h * DV_A:(h + 1) * DV_A] for bb, h in chains]
        wqs = [wq_scr[bb, 2 * ic * c:2 * (ic + 1) * c, h * DK_A:(h + 1) * DK_A] for bb, h in chains]
        qks = [qk_scr[bb, rows, h * LANES:h * LANES + c] for bb, h in chains]
        kds = [kd_scr[bb, rows, h * DK_A:(h + 1) * DK_A] for bb, h in chains]
        egls = [gl_scr[bb, ic * SUBLANES:ic * SUBLANES + 1, h * LANES:(h + 1) * LANES] for bb, h in chains]
        wss = [_dot(wq, s.astype(BF16)) for wq, s in zip(wqs, ss)]
        vns = [(uu - ws[:c]).astype(BF16) for uu, ws in zip(us, wss)]
        ocs = [ws[c:] + _dot(qk, vn) for ws, qk, vn in zip(wss, qks, vns)]
        ss = [s * egl + _dot_tn(kd, vn) for s, egl, kd, vn in zip(ss, egls, kds, vns)]
        for (bb, h), o_c in zip(chains, ocs):
            o_scr[bb, rows, h * DV_A:(h + 1) * DV_A] = o_c
    for (bb, h), s in zip(chains, ss):
        s_scr[bb, h] = s
        s_out_ref[bb, h] = s

    nw = prm_ref[2:3, :]
    for bb in range(nb):
        z = z_ref[bb]
        for h in range(H_A):
            oh = o_scr[bb, 0:tc, h * DV_A:(h + 1) * DV_A]
            zh = z[:, h * DV_A:(h + 1) * DV_A]
            oh = oh * lax.rsqrt(jnp.mean(oh * oh, axis=-1, keepdims=True) + RMS_EPS)
            o_ref[bb, :, h * DV_A:(h + 1) * DV_A] = oh * nw * _silu(zh)


def delta_mixer(x, h, wba, conv_a, prm, s0, db0, layer):
    b, l, _ = x.shape
    tc = min(256, l)
    cp = max(tc, CHUNK_A)
    nt = l // tc
    nb = min(b, max(1, PREP_GROUP // (H_A * (cp // CHUNK_A))) if l < CHUNK_A else 2)
    assert b % nb == 0
    kern = functools.partial(_delta_kernel, nb=nb, tc=tc, cp=cp)
    return pl.pallas_call(
        kern,
        grid=(b // nb, nt),
        in_specs=[pl.BlockSpec((nb, tc, D_MODEL), lambda i, t: (i, t, 0)),
                  pl.BlockSpec((nb, tc, QKVA_W), lambda i, t: (i, t, QKVA_OFF // QKVA_W)),
                  pl.BlockSpec((nb, tc, Z_W), lambda i, t: (i, t, Z_OFF // Z_W)),
                  pl.BlockSpec((None, D_MODEL, LANES), lambda i, t: (layer, 0, 0)),
                  pl.BlockSpec((None, CONV_A, W_QKV_A), lambda i, t: (layer, 0, 0)),
                  pl.BlockSpec((None, SUBLANES, LANES), lambda i, t: (layer, 0, 0)),
                  pl.BlockSpec((nb, H_A, DK_A, DV_A), lambda i, t: (i, 0, 0, 0)),
                  pl.BlockSpec((nb, HALO, W_QKV_A), lambda i, t: (i, 0, 0))],
        out_specs=[pl.BlockSpec((nb, tc, W_V_A), lambda i, t: (i, t, 0)),
                   pl.BlockSpec((nb, H_A, DK_A, DV_A), lambda i, t: (i, 0, 0, 0)),
                   pl.BlockSpec((nb, HALO, W_QKV_A), lambda i, t: (i, 0, 0))],
        out_shape=[jax.ShapeDtypeStruct((b, l, W_V_A), F32),
                   jax.ShapeDtypeStruct((b, H_A, DK_A, DV_A), F32),
                   jax.ShapeDtypeStruct((b, HALO, W_QKV_A), F32)],
        scratch_shapes=[pltpu.VMEM((nb, H_A, DK_A, DV_A), F32),
                        pltpu.VMEM((nb, tc + HALO, W_QKV_A), F32),
                        pltpu.VMEM((nb, cp, W_QK_A), F32),
                        pltpu.VMEM((nb, cp, W_QK_A), F32),
                        pltpu.VMEM((nb, cp, W_V_A), F32),
                        pltpu.VMEM((nb, cp, LANES), F32),
                        pltpu.VMEM((nb, cp, W_V_A), F32),
                        pltpu.VMEM((nb, cp, W_V_A), F32),
                        pltpu.VMEM((nb, 2 * cp, W_QK_A), BF16),
                        pltpu.VMEM((nb, cp, H_A * LANES), BF16),
                        pltpu.VMEM((nb, cp, W_QK_A), BF16),
                        pltpu.VMEM((nb, cp // CHUNK_A * SUBLANES, H_A * LANES), F32)],
        compiler_params=_cparams(("parallel", "arbitrary")),
        name="delta_mixer",
    )(x, h, h, wba, conv_a, prm, s0, db0)


def _sb_group(tiles, lm, bias, r0, chained, first_key_cols=1):
    zs, cs, tots = _sb_scores([(t[0], t[1], t[3]) for t in tiles], lm, bias, first_key_cols)
    return _sb_apply(zs, cs, tots, [t[2] for t in tiles], [t[3] for t in tiles], r0, chained)


def _sb_scores(tiles, lm, bias, first_key_cols):
    zs = [_dot_nt(q, kb) + bias for q, kb, _ in tiles]
    cs, tots = _sb_cumulate(zs, [t[2] for t in tiles], lm, first_key_cols)
    return zs, cs, tots


def _sb_cumulate(zs, valids, lm, first_key_cols):
    sps = [jnp.maximum(z, 0.0) + jnp.log(1.0 + jnp.exp(_neg_abs(z))) for z in zs]
    sps = [sp if v is None else jnp.where(v, sp, 0.0) for sp, v in zip(sps, valids)]
    later = [_dot(sp.astype(BF16), lm) for sp in sps]
    tots = [lt[:, 0:1] + jnp.sum(sp[:, 0:first_key_cols], axis=-1, keepdims=True) for lt, sp in zip(later, sps)]
    return [sp + lt for sp, lt in zip(sps, later)], tots


def _sb_apply(zs, cs, tots, vbs, valids, r0, chained):
    rs, ws = [], []
    r = r0
    for i, (z, c, tot, valid) in enumerate(zip(zs, cs, tots, valids)):
        r_prev = r if chained else r0[i]
        a = jnp.exp(z - c - r_prev)
        if valid is not None:
            a = jnp.where(valid, a, 0.0)
        ws.append(a.astype(BF16))
        r = r_prev + tot
        rs.append(r)
    return [_dot(a, vb) for a, vb in zip(ws, vbs)], rs


def _attn_prompt_kernel(bias_ref, q_ref, k_ref, v_ref, lm_ref, o_ref, kbf, vbf, acc, rsum, zbuf, cbuf, tbuf,
                        *, tq, tk, ts):
    h = pl.program_id(1)
    qi = pl.program_id(2)
    nsub = tq // ts
    ndiag = tq // tk

    @pl.when(qi == 0)
    def _():
        kbf[...] = k_ref[0].astype(BF16)
        vbf[...] = v_ref[0].astype(BF16)

    q = (q_ref[0] * (DH_B ** -0.5)).astype(BF16)
    qs = [q[s * ts:(s + 1) * ts] for s in range(nsub)]
    bias = bias_ref[h]
    lm = lm_ref[...]
    row = lax.broadcasted_iota(jnp.int32, (ts, tk), 0)
    col = lax.broadcasted_iota(jnp.int32, (ts, tk), 1)
    started = set()

    def visit(k0, rel):
        kb = kbf[pl.ds(k0, tk), :]
        vb = vbf[pl.ds(k0, tk), :]
        subs, tiles, r0 = [], [], []
        for s in range(nsub):
            off = tk if rel is None else s * ts - rel
            if off + ts - 1 <= 0:
                continue
            valid = None if off >= tk else col < row + off
            subs.append(s)
            tiles.append((qs[s], kb, vb, valid))
            r0.append(rsum[s * ts:(s + 1) * ts, :] if s in started else jnp.zeros((ts, 1), F32))
        old = [acc[s * ts:(s + 1) * ts, :] if s in started else None for s in subs]
        pvs, rs = _sb_group(tiles, lm, bias, r0, False)
        for s, o, pv, r in zip(subs, old, pvs, rs):
            acc[s * ts:(s + 1) * ts, :] = pv if o is None else o + pv
            rsum[s * ts:(s + 1) * ts, :] = r
            started.add(s)

    q0 = qi * tq
    for d in reversed(range(ndiag)):
        visit(pl.multiple_of(q0 + d * tk, tk), d * tk)

    n_past = qi * ndiag

    def past_block(i):
        return pl.multiple_of(jnp.maximum(q0 - (i + 1) * tk, 0), tk)

    rows = [slice(s * ts, (s + 1) * ts) for s in range(nsub)]

    def qk(i):
        kb = kbf[pl.ds(past_block(i), tk), :]
        return [_dot_nt(qs[s], kb) + bias for s in range(nsub)]

    def finish_scores(zs, slot):
        cs, tots = _sb_cumulate(zs, [None] * nsub, lm, 1)
        for r, z, c, t in zip(rows, zs, cs, tots):
            zbuf[slot, r, :] = z
            cbuf[slot, r, :] = c
            tbuf[slot, r, :] = t

    def apply(i, slot):
        vb = vbf[pl.ds(past_block(i), tk), :]
        pvs, rs = _sb_apply([zbuf[slot, r, :] for r in rows], [cbuf[slot, r, :] for r in rows],
                            [tbuf[slot, r, :] for r in rows], [vb] * nsub, [None] * nsub,
                            [rsum[r, :] for r in rows], False)
        for r, pv, rn in zip(rows, pvs, rs):
            acc[r, :] += pv
            rsum[r, :] = rn

    @pl.when(n_past > 0)
    def _():
        finish_scores(qk(0), 0)

    def body(j, carry):
        for slot in range(2):
            i = 2 * j + slot
            z_next = qk(i + 1)
            apply(i, slot)
            finish_scores(z_next, 1 - slot)
        return carry

    lax.fori_loop(0, n_past // 2, body, 0)
    o_ref[0] = acc[...]


def _later_ones(n):
    r = lax.broadcasted_iota(jnp.int32, (n, n), 0)
    c = lax.broadcasted_iota(jnp.int32, (n, n), 1)
    return (r > c).astype(BF16)


def attn_prompt(h, sb_bias):
    b, l, _ = h.shape
    tq = min(1024, l)
    tk = min(256, tq)
    ts = min(128, tq)
    assert l == tq or (tq // tk) % 2 == 0
    qb = QKVB_OFF // DH_B
    kern = functools.partial(_attn_prompt_kernel, tq=tq, tk=tk, ts=ts)
    grid_spec = pltpu.PrefetchScalarGridSpec(
        num_scalar_prefetch=0,
        grid=(b, H_B, l // tq),
        in_specs=[pl.BlockSpec(memory_space=pltpu.SMEM),
                  pl.BlockSpec((1, tq, DH_B), lambda i, hh, j: (i, j, qb + hh)),
                  pl.BlockSpec((1, l, DH_B), lambda i, hh, j: (i, 0, qb + H_B + hh)),
                  pl.BlockSpec((1, l, DH_B), lambda i, hh, j: (i, 0, qb + 2 * H_B + hh)),
                  pl.BlockSpec((tk, tk), lambda i, hh, j: (0, 0))],
        out_specs=pl.BlockSpec((1, tq, DH_B), lambda i, hh, j: (i, j, hh)),
        scratch_shapes=[pltpu.VMEM((l, DH_B), BF16),
                        pltpu.VMEM((l, DH_B), BF16),
                        pltpu.VMEM((tq, DH_B), F32),
                        pltpu.VMEM((tq, 1), F32),
                        pltpu.VMEM((2, tq, tk), F32),
                        pltpu.VMEM((2, tq, tk), F32),
                        pltpu.VMEM((2, tq, 1), F32)])
    return pl.pallas_call(
        kern,
        grid_spec=grid_spec,
        out_shape=jax.ShapeDtypeStruct((b, l, W_B), F32),
        compiler_params=_cparams(("parallel", "parallel", "arbitrary")),
        name="attn_prompt",
    )(sb_bias, h, h, h, _later_ones(tk))


def _attn_sample_kernel(pt_ref, bias_ref, qkv_ref, *rest, seq, pages_per_step, page_group, n_steps):
    g_pages = pages_per_step
    k_refs = rest[:g_pages]
    v_refs = rest[g_pages:2 * g_pages]
    lm_ref = rest[2 * g_pages]
    lx_ref = rest[2 * g_pages + 1]
    o_ref = rest[2 * g_pages + 2]
    qall, acc, rsum = rest[2 * g_pages + 3:]
    s = pl.program_id(1)
    rows = H_B * seq
    pcols = PAGE_SIZE * H_B
    rid = lax.broadcasted_iota(jnp.int32, (rows, 1), 0)
    bias = jnp.zeros((rows, 1), F32)
    for hh in range(H_B):
        bias = jnp.where((rid >= hh * seq) & (rid < (hh + 1) * seq), bias_ref[hh], bias)

    @pl.when(s == 0)
    def _():
        qkv = qkv_ref[0]
        q = qkv[:, 0:W_B] * (DH_B ** -0.5)
        qall[...] = jnp.concatenate([q[:, hh * DH_B:(hh + 1) * DH_B] for hh in range(H_B)], axis=0).astype(BF16)
        lane = lax.broadcasted_iota(jnp.int32, (seq, W_B), 1)
        parts = [jnp.where((lane >= hh * DH_B) & (lane < (hh + 1) * DH_B), q, 0.0) for hh in range(H_B)]
        qbd = jnp.concatenate(parts, axis=0).astype(BF16)
        zrows = jnp.zeros((PAGE_SIZE - seq, W_B), F32)
        k_own = jnp.concatenate([qkv[:, W_B:2 * W_B], zrows], axis=0).astype(BF16)
        v_own = jnp.concatenate([qkv[:, 2 * W_B:3 * W_B], zrows], axis=0).astype(BF16)
        key = lax.broadcasted_iota(jnp.int32, (rows, PAGE_SIZE), 1)
        qpos = lax.broadcasted_iota(jnp.int32, (rows, PAGE_SIZE), 0) % seq
        pvs, rs = _sb_group([(qbd, k_own, v_own, key < qpos)], lm_ref[...], bias, jnp.zeros((rows, 1), F32), True)
        acc[...] = jnp.concatenate(
            [pvs[0][hh * seq:(hh + 1) * seq, hh * DH_B:(hh + 1) * DH_B] for hh in range(H_B)], axis=0)
        rsum[...] = rs[0]

    own = (lax.broadcasted_iota(jnp.int32, (rows, pcols), 1) % H_B
           == lax.broadcasted_iota(jnp.int32, (rows, pcols), 0) // seq)
    qa = qall[...]
    lx = lx_ref[...]
    r = rsum[...]
    total = acc[...]
    for g0 in reversed(range(0, g_pages, page_group)):
        tiles = [(qa, k_refs[gi][0, 0].astype(BF16), v_refs[gi][0, 0].astype(BF16), own)
                 for gi in reversed(range(g0, g0 + page_group))]
        pvs, rs = _sb_group(tiles, lx, bias, r, True, H_B)
        r = rs[-1]
        for pv in pvs:
            total = total + pv
    acc[...] = total
    rsum[...] = r

    @pl.when(s == n_steps - 1)
    def _():
        a = acc[...]
        o_ref[0] = jnp.concatenate([a[hh * seq:(hh + 1) * seq, :] for hh in range(H_B)], axis=1)


def attn_sample(h, cache_k, cache_v, page_table, sb_bias, layer):
    b, seq, _ = h.shape
    depth, n_pool = cache_k.shape[:2]
    n_pages = page_table.shape[0] // b
    g_pages = min(16, n_pages)
    n_steps = n_pages // g_pages
    rows = H_B * seq
    pcols = PAGE_SIZE * H_B
    ck = cache_k.reshape(depth, n_pool, pcols, DH_B)
    cv = cache_v.reshape(depth, n_pool, pcols, DH_B)
    kidx = lax.broadcasted_iota(jnp.int32, (pcols, pcols), 0) // H_B
    lexp = (kidx > kidx.T).astype(BF16)

    def page_map(gi):
        def index_map(i, s, pt):
            return (layer, pt[i * n_pages + (n_steps - 1 - s) * g_pages + gi], 0, 0)
        return index_map

    page_specs = [pl.BlockSpec((1, 1, pcols, DH_B), page_map(gi)) for gi in range(g_pages)]
    kern = functools.partial(_attn_sample_kernel, seq=seq, pages_per_step=g_pages, page_group=min(4, g_pages),
                             n_steps=n_steps)
    grid_spec = pltpu.PrefetchScalarGridSpec(
        num_scalar_prefetch=1,
        grid=(b, n_steps),
        in_specs=([pl.BlockSpec(memory_space=pltpu.SMEM),
                   pl.BlockSpec((1, seq, QKVB_W), lambda i, s, pt: (i, 0, QKVB_OFF // QKVB_W))]
                  + page_specs + page_specs
                  + [pl.BlockSpec((PAGE_SIZE, PAGE_SIZE), lambda i, s, pt: (0, 0)),
                     pl.BlockSpec((pcols, pcols), lambda i, s, pt: (0, 0))]),
        out_specs=pl.BlockSpec((1, seq, W_B), lambda i, s, pt: (i, 0, 0)),
        scratch_shapes=[pltpu.VMEM((rows, DH_B), BF16),
                        pltpu.VMEM((rows, DH_B), F32),
                        pltpu.VMEM((rows, 1), F32)])
    return pl.pallas_call(
        kern,
        grid_spec=grid_spec,
        out_shape=jax.ShapeDtypeStruct((b, seq, W_B), F32),
        compiler_params=_cparams(("parallel", "arbitrary")),
        name="attn_sample",
    )(page_table, sb_bias, h, *([ck] * g_pages), *([cv] * g_pages), _later_ones(PAGE_SIZE), lexp)


def _merge_kernel(x_ref, g_ref, c_ref, oa_ref, ob_ref, wpa_ref, wpb_ref, wpc_ref, wo_ref, conv_ref, ln_ref, sb0_ref,
                  o_ref, tail_ref, ext_scr, *, tm):
    t = pl.program_id(1)

    @pl.when(t == 0)
    def _():
        ext_scr[0:HALO, :] = sb0_ref[0]

    cc = c_ref[0]
    u = cc[:, 2 * W_C:] * cc[:, :W_C]
    ext_scr[HALO:HALO + tm, :] = u
    w = conv_ref[...]
    y = ext_scr[HALO - 2:HALO - 2 + tm, :] * w[0:1, :] + ext_scr[HALO - 1:HALO - 1 + tm, :] * w[1:2, :] + u * w[2:3, :]
    tail = ext_scr[tm:tm + HALO, :]
    ext_scr[0:HALO, :] = tail
    tail_ref[0] = tail
    o_c = cc[:, W_C:2 * W_C] * y

    g = g_ref[0]
    merged = (_sigmoid(g[:, :D_MODEL]) * _dot(oa_ref[0].astype(BF16), wpa_ref[...])
              + _sigmoid(g[:, D_MODEL:2 * D_MODEL]) * _dot(ob_ref[0].astype(BF16), wpb_ref[...])
              + _sigmoid(g[:, 2 * D_MODEL:]) * _dot(o_c.astype(BF16), wpc_ref[...]))
    r = DEEPNORM_ALPHA * x_ref[0] + _dot(merged.astype(BF16), wo_ref[...])
    o_ref[0] = _layer_norm(r, ln_ref[0:1, :], ln_ref[1:2, :])


def merge(x, h, o_a, o_b, wpa, wpb, wpc, wo, conv_c, ln, sb0, layer):
    b, l, _ = x.shape
    tm = min(512, l)
    kern = functools.partial(_merge_kernel, tm=tm)
    const = lambda i, t: (layer, 0, 0)
    return pl.pallas_call(
        kern,
        grid=(b, l // tm),
        in_specs=[pl.BlockSpec((1, tm, D_MODEL), lambda i, t: (i, t, 0)),
                  pl.BlockSpec((1, tm, G_W), lambda i, t: (i, t, G_OFF // G_W)),
                  pl.BlockSpec((1, tm, C_W), lambda i, t: (i, t, C_OFF // C_W)),
                  pl.BlockSpec((1, tm, W_V_A), lambda i, t: (i, t, 0)),
                  pl.BlockSpec((1, tm, W_B), lambda i, t: (i, t, 0)),
                  pl.BlockSpec((None, W_V_A, D_MODEL), const),
                  pl.BlockSpec((None, W_B, D_MODEL), const),
                  pl.BlockSpec((None, W_C, D_MODEL), const),
                  pl.BlockSpec((None, D_MODEL, D_MODEL), const),
                  pl.BlockSpec((None, CONV_C, W_C), const),
                  pl.BlockSpec((None, 2, D_MODEL), const),
                  pl.BlockSpec((1, HALO, W_C), lambda i, t: (i, 0, 0))],
        out_specs=[pl.BlockSpec((1, tm, D_MODEL), lambda i, t: (i, t, 0)),
                   pl.BlockSpec((1, HALO, W_C), lambda i, t: (i, 0, 0))],
        out_shape=[jax.ShapeDtypeStruct((b, l, D_MODEL), F32),
                   jax.ShapeDtypeStruct((b, HALO, W_C), F32)],
        scratch_shapes=[pltpu.VMEM((tm + HALO, W_C), F32)],
        compiler_params=_cparams(("parallel", "arbitrary")),
        name="merge",
    )(x, h, h, o_a, o_b, wpa, wpb, wpc, wo, conv_c, ln, sb0)


def _moe_kernel(x_ref, wr_ref, br_ref, wgu_ref, wdn_ref, ln_ref, o_ref, xb, gates, acc, *, tm, eps):
    e = pl.program_id(1)
    lane = lax.broadcasted_iota(jnp.int32, (tm, LANES), 1)

    @pl.when(e == 0)
    def _():
        x = x_ref[...]
        xh, xl = _split2(x)
        wh, wl = _split2(wr_ref[...])
        logits = _dot(xh, wh) + _dot(xh, wl) + _dot(xl, wh) + br_ref[...]
        lanef = lane.astype(F32)
        big = float(LANES)
        is_g = lane < N_GROUPS
        gl = jnp.where(is_g, logits, -jnp.inf)
        gmax = jnp.max(gl, axis=-1, keepdims=True)
        gsel = jnp.min(jnp.where(gl == gmax, lanef, big), axis=-1, keepdims=True)
        pg_sel = 1.0 / jnp.sum(jnp.where(is_g, jnp.exp(gl - gmax), 0.0), axis=-1, keepdims=True)
        lo = N_GROUPS + gsel * EXPERTS_PER_GROUP
        ev = jnp.where((lanef >= lo) & (lanef < lo + EXPERTS_PER_GROUP), logits, -jnp.inf)
        v1 = jnp.max(ev, axis=-1, keepdims=True)
        i1 = jnp.min(jnp.where(ev == v1, lanef, big), axis=-1, keepdims=True)
        ev2 = jnp.where(lanef == i1, -jnp.inf, ev)
        v2 = jnp.max(ev2, axis=-1, keepdims=True)
        i2 = jnp.min(jnp.where(ev2 == v2, lanef, big), axis=-1, keepdims=True)
        e2 = jnp.exp(v2 - v1)
        den = 1.0 + e2
        gates[...] = jnp.where(lanef == i1, pg_sel / den, jnp.where(lanef == i2, pg_sel * e2 / den, 0.0))
        xb[...] = xh
        acc[...] = jnp.zeros((tm, D_MODEL), F32)

    xv = xb[...]
    g_all = gates[...]
    hhs = [_dot(xv, wgu_ref[j]) for j in range(eps)]
    acts = []
    for j, hh in enumerate(hhs):
        ge = jnp.sum(jnp.where(lane == N_GROUPS + e * eps + j, g_all, 0.0), axis=-1, keepdims=True)
        acts.append((_silu(hh[:, :D_EXPERT]) * hh[:, D_EXPERT:] * ge).astype(BF16))
    acc[...] += _dot(jnp.concatenate(acts, axis=1), wdn_ref[...].reshape(eps * D_EXPERT, D_MODEL))

    @pl.when(e == N_EXPERTS // eps - 1)
    def _():
        r = DEEPNORM_ALPHA * x_ref[...] + acc[...]
        o_ref[...] = _layer_norm(r, ln_ref[0:1, :], ln_ref[1:2, :])


def moe(x2d, wr, br, wgu, wdn, ln, layer):
    t_rows = x2d.shape[0]
    tm = min(1024, t_rows)
    eps = 2
    kern = functools.partial(_moe_kernel, tm=tm, eps=eps)
    return pl.pallas_call(
        kern,
        grid=(t_rows // tm, N_EXPERTS // eps),
        in_specs=[pl.BlockSpec((tm, D_MODEL), lambda i, e: (i, 0)),
                  pl.BlockSpec((None, D_MODEL, LANES), lambda i, e: (layer, 0, 0)),
                  pl.BlockSpec((None, 1, LANES), lambda i, e: (layer, 0, 0)),
                  pl.BlockSpec((eps, D_MODEL, 2 * D_EXPERT), lambda i, e: (layer * (N_EXPERTS // eps) + e, 0, 0)),
                  pl.BlockSpec((eps, D_EXPERT, D_MODEL), lambda i, e: (layer * (N_EXPERTS // eps) + e, 0, 0)),
                  pl.BlockSpec((None, 2, D_MODEL), lambda i, e: (layer, 0, 0))],
        out_specs=pl.BlockSpec((tm, D_MODEL), lambda i, e: (i, 0)),
        out_shape=jax.ShapeDtypeStruct((t_rows, D_MODEL), F32),
        scratch_shapes=[pltpu.VMEM((tm, D_MODEL), BF16),
                        pltpu.VMEM((tm, LANES), F32),
                        pltpu.VMEM((tm, D_MODEL), F32)],
        compiler_params=_cparams(("parallel", "arbitrary")),
        name="moe",
    )(x2d, wr, br, wgu, wdn, ln)


def _prep_params(w_in, conv_a, a_log, dt_bias, norm_a, conv_c, w_pa, w_pb, w_pc, w_o, ln_g, ln_b,
                 w_rg, b_rg, w_re, b_re, w_gu, w_down):
    depth = w_in.shape[0]
    o_z = W_QKV_A
    o_ba = o_z + W_V_A
    o_qb = o_ba + 2 * H_A
    o_c = o_qb + 3 * W_B
    o_g = o_c + 3 * W_C
    w_main = jnp.concatenate([w_in[:, :, o_g:], w_in[:, :, :W_QKV_A], w_in[:, :, o_c:o_g], w_in[:, :, o_qb:o_c],
                              w_in[:, :, o_z:o_ba]], axis=2).astype(BF16)
    w_ba = jnp.pad(w_in[:, :, o_ba:o_qb], ((0, 0), (0, 0), (0, LANES - 2 * H_A))).astype(BF16)
    prm = jnp.zeros((depth, SUBLANES, LANES), F32)
    prm = prm.at[:, 0, H_A:2 * H_A].set(a_log).at[:, 1, H_A:2 * H_A].set(dt_bias).at[:, 2, :DV_A].set(norm_a)
    pad_r = LANES - N_GROUPS - N_EXPERTS
    w_r = jnp.pad(jnp.concatenate([w_rg, w_re], axis=2), ((0, 0), (0, 0), (0, pad_r)))
    b_r = jnp.pad(jnp.concatenate([b_rg, b_re], axis=1), ((0, 0), (0, pad_r))).reshape(depth, 1, LANES)
    return dict(
        w_main=w_main, w_ba=w_ba, conv_a=conv_a, prm=prm, conv_c=conv_c,
        w_pa=w_pa.astype(BF16), w_pb=w_pb.astype(BF16), w_pc=w_pc.astype(BF16), w_o=w_o.astype(BF16),
        ln0=jnp.stack([ln_g[:, 0], ln_b[:, 0]], axis=1), ln1=jnp.stack([ln_g[:, 1], ln_b[:, 1]], axis=1),
        w_r=w_r, b_r=b_r,
        w_gu=w_gu.astype(BF16).reshape(depth * N_EXPERTS, D_MODEL, 2 * D_EXPERT),
        w_down=w_down.astype(BF16).reshape(depth * N_EXPERTS, D_EXPERT, D_MODEL))


def _pad_tail(buf):
    return jnp.pad(buf, ((0, 0), (HALO - buf.shape[1], 0), (0, 0)))


def _layer(x, p, s0, db0, sb0, attn_fn, layer, depth, k_buf, v_buf):
    b, l, _ = x.shape
    h, k_buf, v_buf = in_proj(x.reshape(b * l, D_MODEL), p["w_main"], layer, depth, k_buf, v_buf)
    h = h.reshape(b, l, H_COLS)
    o_a, s_new, dtail = delta_mixer(x, h, p["w_ba"], p["conv_a"], p["prm"], s0, _pad_tail(db0), layer)
    o_b = attn_fn(h)
    x1, stail = merge(x, h, o_a, o_b, p["w_pa"], p["w_pb"], p["w_pc"], p["w_o"], p["conv_c"], p["ln0"],
                      _pad_tail(sb0), layer)
    x2 = moe(x1.reshape(b * l, D_MODEL), p["w_r"], p["b_r"], p["w_gu"], p["w_down"], p["ln1"], layer)
    x2 = x2.reshape(b, l, D_MODEL)
    return (x2, s_new, dtail[:, HALO - (CONV_A - 1):], stail[:, HALO - (CONV_C - 1):]), k_buf, v_buf


def kernel(x_prompt, x_sample, cache_k, cache_v, state_delta, state_dconv, state_sconv, page_table, w_in, conv_a, a_log, dt_bias, norm_a, sb_bias, conv_c, w_pa, w_pb, w_pc, w_o, ln_g, ln_b, w_rg, b_rg, w_re, b_re, w_gu, w_down):
    bp, lp, _ = x_prompt.shape
    bs, ls, _ = x_sample.shape
    depth = w_in.shape[0]
    pt = page_table.reshape(-1).astype(jnp.int32)
    xp, xs = x_prompt, x_sample
    outs_p = [[] for _ in range(3)]
    outs_s = [[] for _ in range(3)]
    pk = pv = sk = sv = None
    p = _prep_params(w_in, conv_a, a_log, dt_bias, norm_a, conv_c, w_pa, w_pb, w_pc, w_o, ln_g, ln_b,
                     w_rg, b_rg, w_re, b_re, w_gu, w_down)
    for l in range(depth):
        bias = sb_bias[l]
        res, pk, pv = _layer(xp, p,
                             jnp.zeros((bp, H_A, DK_A, DV_A), F32),
                             jnp.zeros((bp, CONV_A - 1, W_QKV_A), F32),
                             jnp.zeros((bp, CONV_C - 1, W_C), F32),
                             lambda h: attn_prompt(h, bias), l, depth, pk, pv)
        xp = res[0]
        for acc_list, r in zip(outs_p, res[1:]):
            acc_list.append(r)
        res, sk, sv = _layer(xs, p, state_delta[l], state_dconv[l], state_sconv[l],
                             lambda h: attn_sample(h, cache_k, cache_v, pt, bias, l), l, depth, sk, sv)
        xs = res[0]
        for acc_list, r in zip(outs_s, res[1:]):
            acc_list.append(r)
    return (xp, xs, *[jnp.stack(o) for o in outs_p],
            pk.reshape(depth, bp, lp, H_B, DH_B), pv.reshape(depth, bp, lp, H_B, DH_B),
            *[jnp.stack(o) for o in outs_s],
            sk.reshape(depth, bs, ls, H_B, DH_B), sv.reshape(depth, bs, ls, H_B, DH_B))
```

```python
import functools

import jax
import jax.numpy as jnp
from jax import lax
from jax.experimental import pallas as pl
from jax.experimental.pallas import tpu as pltpu

F32 = jnp.float32
BF16 = jnp.bfloat16

D_MODEL = 1024
DEPTH = 4
H_A = 4
DK_A = 128
DV_A = 128
CONV_A = 4
CHUNK_A = 64
H_B = 4
DH_B = 128
W_C = 512
CONV_C = 3
N_GROUPS = 4
EXPERTS_PER_GROUP = 4
N_EXPERTS = N_GROUPS * EXPERTS_PER_GROUP
D_EXPERT = 256
PAGE_SIZE = 128

W_QK_A = H_A * DK_A
W_V_A = H_A * DV_A
W_QKV_A = 2 * W_QK_A + W_V_A
W_B = H_B * DH_B

DEEPNORM_ALPHA = (2.0 * DEPTH) ** 0.25
LN_EPS = 1e-5
RMS_EPS = 1e-6

G_OFF, G_W = 0, 3 * D_MODEL
QKVA_OFF, QKVA_W = 3072, W_QKV_A
C_OFF, C_W = 4608, 3 * W_C
QKVB_OFF, QKVB_W = 6144, 3 * W_B
Z_OFF, Z_W = 7680, W_V_A
H_COLS = 8192
LANES = 128
SUBLANES = 8
HALO = SUBLANES
PREP_GROUP = 16

VMEM_LIMIT = 56 * 1024 * 1024


def _cparams(sem):
    return pltpu.CompilerParams(dimension_semantics=sem, vmem_limit_bytes=VMEM_LIMIT)


def _dot(a, b):
    return jnp.dot(a, b, preferred_element_type=F32)


def _dot_nt(a, b):
    return lax.dot_general(a, b, (((1,), (1,)), ((), ())), preferred_element_type=F32)


def _dot_tn(a, b):
    return lax.dot_general(a, b, (((0,), (0,)), ((), ())), preferred_element_type=F32)


def _split2(x):
    hi = x.astype(BF16)
    lo = (x - hi.astype(F32)).astype(BF16)
    return hi, lo


def _split3(x):
    hi = x.astype(BF16)
    r = x - hi.astype(F32)
    mid = r.astype(BF16)
    lo = (r - mid.astype(F32)).astype(BF16)
    return hi, mid, lo


def _mm_sel(sel, x):
    hi, mid, lo = _split3(x)
    return _dot(sel, hi) + _dot(sel, mid) + _dot(sel, lo)


def _neg_abs(x):
    bits = lax.bitcast_convert_type(x, jnp.uint32) | jnp.uint32(0x80000000)
    return lax.bitcast_convert_type(bits, F32)


def _softplus(x):
    return jnp.maximum(x, 0.0) + jnp.log1p(jnp.exp(-jnp.abs(x)))


def _sigmoid(x):
    return 1.0 / (1.0 + jnp.exp(-x))


def _silu(x):
    return x * _sigmoid(x)


def _layer_norm(r, g, b):
    mu = jnp.mean(r, axis=-1, keepdims=True)
    d = r - mu
    var = jnp.mean(d * d, axis=-1, keepdims=True)
    return d * lax.rsqrt(var + LN_EPS) * g + b


K_COL = QKVB_OFF + W_B
V_COL = QKVB_OFF + 2 * W_B


def _inproj_kernel(x_ref, w_ref, *refs, tm, tn):
    o_ref, k_ref, v_ref = refs[-3:]
    j = pl.program_id(1)
    o_ref[...] = _dot(x_ref[...].astype(BF16), w_ref[...])

    def rows_out(ref, col):
        @pl.when(j == col // tn)
        def _():
            for hh in range(H_B):
                c0 = col % tn + hh * DH_B
                ref[0, pl.ds(hh, tm, stride=H_B), :] = o_ref[:, c0:c0 + DH_B]

    rows_out(k_ref, K_COL)
    rows_out(v_ref, V_COL)


def in_proj(x2d, w, layer, depth, k_buf, v_buf):
    t_rows = x2d.shape[0]
    tm = min(1024, t_rows)
    tn = 2048
    assert K_COL % tn + W_B <= tn and V_COL % tn + W_B <= tn
    rows_shape = jax.ShapeDtypeStruct((depth, t_rows * H_B, DH_B), F32)
    rows_spec = pl.BlockSpec((1, tm * H_B, DH_B), lambda i, j: (layer, i, 0))
    carried = [] if k_buf is None else [k_buf, v_buf]
    return pl.pallas_call(
        functools.partial(_inproj_kernel, tm=tm, tn=tn),
        grid=(t_rows // tm, H_COLS // tn),
        in_specs=[pl.BlockSpec((tm, D_MODEL), lambda i, j: (i, 0)),
                  pl.BlockSpec((None, D_MODEL, tn), lambda i, j: (layer, 0, j))]
                 + [pl.BlockSpec(memory_space=pl.ANY)] * len(carried),
        out_specs=[pl.BlockSpec((tm, tn), lambda i, j: (i, j)), rows_spec, rows_spec],
        out_shape=[jax.ShapeDtypeStruct((t_rows, H_COLS), F32), rows_shape, rows_shape],
        input_output_aliases={2: 1, 3: 2} if carried else {},
        compiler_params=_cparams(("parallel", "arbitrary")),
        name="in_proj",
    )(x2d, w, *carried)


def _delta_kernel(x_ref, qkv_ref, z_ref, wba_ref, conv_ref, prm_ref, s0_ref, db0_ref,
                  o_ref, s_out_ref, tail_ref,
                  s_scr, ext_scr, q_scr, k_scr, v_scr, bg_scr, o_scr, u_scr, wq_scr, qk_scr, kd_scr, gl_scr,
                  *, nb, tc, cp):
    c = CHUNK_A
    t = pl.program_id(1)

    @pl.when(t == 0)
    def _():
        s_scr[...] = s0_ref[...]
        ext_scr[:, 0:HALO, :] = db0_ref[...]

    w = conv_ref[...]
    for bb in range(nb):
        u = qkv_ref[bb]
        ext_scr[bb, HALO:HALO + tc, :] = u
        y = (ext_scr[bb, HALO - 3:HALO - 3 + tc, :] * w[0:1, :] + ext_scr[bb, HALO - 2:HALO - 2 + tc, :] * w[1:2, :]
             + ext_scr[bb, HALO - 1:HALO - 1 + tc, :] * w[2:3, :] + u * w[3:4, :])
        tail = ext_scr[bb, tc:tc + HALO, :]
        ext_scr[bb, 0:HALO, :] = tail
        tail_ref[bb] = tail
        y = _silu(y)

        ba = _dot(x_ref[bb].astype(BF16), wba_ref[...])
        beta = _sigmoid(ba)
        g = -jnp.exp(prm_ref[0:1, :]) * _softplus(ba + prm_ref[1:2, :])
        lane = lax.broadcasted_iota(jnp.int32, (tc, LANES), 1)
        bg = jnp.where(lane < H_A, beta, g)

        if cp > tc:
            q_scr[bb, tc:cp, :] = jnp.zeros((cp - tc, W_QK_A), F32)
            k_scr[bb, tc:cp, :] = jnp.zeros((cp - tc, W_QK_A), F32)
            v_scr[bb, tc:cp, :] = jnp.zeros((cp - tc, W_V_A), F32)
            bg_scr[bb, tc:cp, :] = jnp.zeros((cp - tc, LANES), F32)
        bg_scr[bb, 0:tc, :] = bg
        for h in range(H_A):
            qh = y[:, h * DK_A:(h + 1) * DK_A]
            kh = y[:, W_QK_A + h * DK_A:W_QK_A + (h + 1) * DK_A]
            qn = qh * lax.rsqrt(jnp.sum(qh * qh, axis=-1, keepdims=True) + RMS_EPS) * (DK_A ** -0.5)
            kn = kh * lax.rsqrt(jnp.sum(kh * kh, axis=-1, keepdims=True) + RMS_EPS)
            q_scr[bb, 0:tc, h * DK_A:(h + 1) * DK_A] = qn
            k_scr[bb, 0:tc, h * DK_A:(h + 1) * DK_A] = kn
        v_scr[bb, 0:tc, :] = y[:, 2 * W_QK_A:]

    ri = lax.broadcasted_iota(jnp.int32, (c, c), 0)
    ci = lax.broadcasted_iota(jnp.int32, (c, c), 1)
    tril = (ri >= ci).astype(BF16)
    triu_f = (ri <= ci).astype(F32)
    ones = jnp.ones((c, c), BF16)

    def prep_many(ins):
        qcs, kcs, vcs, betas, gs = zip(*ins)
        gbs = [jnp.broadcast_to(g_c, (c, LANES)) for g_c in gs]
        gcs = [_mm_sel(tril, gb) for gb in gbs]
        grows = [_mm_sel(ones, gb[:, :c] * triu_f) for gb in gbs]
        decs = [jnp.exp(jnp.where(ri >= ci, gc[:, :c] - gr, -jnp.inf)) for gc, gr in zip(gcs, grows)]
        kbs = [kc * b for kc, b in zip(kcs, betas)]
        kqs = [_dot_nt(jnp.concatenate([kb, qc], axis=0).astype(BF16), kc.astype(BF16))
               for kb, qc, kc in zip(kbs, qcs, kcs)]
        qks = [kq[c:] * dec for kq, dec in zip(kqs, decs)]
        egcs = [jnp.exp(gc) for gc in gcs]
        rhss = [jnp.concatenate([vc * b, kb * egc], axis=1) for vc, b, kb, egc in zip(vcs, betas, kbs, egcs)]
        ns = [-jnp.where(ri > ci, kq[:c] * dec, 0.0) for kq, dec in zip(kqs, decs)]
        yys = ns
        for _ in range(5):
            nbs = [n.astype(BF16) for n in ns]
            ns = [_dot(nb, nb) for nb in nbs]
            prods = [_dot(yy.astype(BF16), n.astype(BF16)) for yy, n in zip(yys, ns)]
            yys = [yy + n + p for yy, n, p in zip(yys, ns, prods)]
        sols = [rhs + _dot(yy.astype(BF16), rhs.astype(BF16)) for rhs, yy in zip(rhss, yys)]
        gls = [gc[c - 1:c, :] for gc in gcs]
        outs = []
        for sol, qc, kc, qk, egc, gc, gl in zip(sols, qcs, kcs, qks, egcs, gcs, gls):
            wq = jnp.concatenate([sol[:, DV_A:], qc * egc], axis=0).astype(BF16)
            k_dec = (kc * jnp.exp(gl - gc)).astype(BF16)
            outs.append((sol[:, :DV_A], wq, qk.astype(BF16), k_dec,
                         jnp.broadcast_to(jnp.exp(gl), (SUBLANES, LANES))))
        return outs

    n_chunks = cp // c
    problems = [(bb, ic, h) for bb in range(nb) for ic in range(n_chunks) for h in range(H_A)]
    for p0 in range(0, len(problems), PREP_GROUP):
        group = problems[p0:p0 + PREP_GROUP]
        ins = []
        for bb, ic, h in group:
            rows = slice(ic * c, (ic + 1) * c)
            ins.append((q_scr[bb, rows, h * DK_A:(h + 1) * DK_A], k_scr[bb, rows, h * DK_A:(h + 1) * DK_A],
                        v_scr[bb, rows, h * DV_A:(h + 1) * DV_A], bg_scr[bb, rows, h:h + 1],
                        bg_scr[bb, rows, H_A + h:H_A + h + 1]))
        for (bb, ic, h), (uu, wq, qk, k_dec, egl) in zip(group, prep_many(ins)):
            rows = slice(ic * c, (ic + 1) * c)
            u_scr[bb, rows, h * DV_A:(h + 1) * DV_A] = uu
            wq_scr[bb, 2 * ic * c:2 * (ic + 1) * c, h * DK_A:(h + 1) * DK_A] = wq
            qk_scr[bb, rows, h * LANES:h * LANES + c] = qk
            kd_scr[bb, rows, h * DK_A:(h + 1) * DK_A] = k_dec
            gl_scr[bb, ic * SUBLANES:(ic + 1) * SUBLANES, h * LANES:(h + 1) * LANES] = egl

    chains = [(bb, h) for bb in range(nb) for h in range(H_A)]
    ss = [s_scr[bb, h] for bb, h in chains]
    for ic in range(n_chunks):
        rows = slice(ic * c, (ic + 1) * c)
        us = [u_scr[bb, rows, h * DV_A:(h + 1) * DV_A] for bb, h in chains]
        wqs = [wq_scr[bb, 2 * ic * c:2 * (ic + 1) * c, h * DK_A:(h + 1) * DK_A] for bb, h in chains]
        qks = [qk_scr[bb, rows, h * LANES:h * LANES + c] for bb, h in chains]
        kds = [kd_scr[bb, rows, h * DK_A:(h + 1) * DK_A] for bb, h in chains]
        egls = [gl_scr[bb, ic * SUBLANES:ic * SUBLANES + 1, h * LANES:(h + 1) * LANES] for bb, h in chains]
        wss = [_dot(wq, s.astype(BF16)) for wq, s in zip(wqs, ss)]
        vns = [(uu - ws[:c]).astype(BF16) for uu, ws in zip(us, wss)]
        ocs = [ws[c:] + _dot(qk, vn) for ws, qk, vn in zip(wss, qks, vns)]
        ss = [s * egl + _dot_tn(kd, vn) for s, egl, kd, vn in zip(ss, egls, kds, vns)]
        for (bb, h), o_c in zip(chains, ocs):
            o_scr[bb, rows, h * DV_A:(h + 1) * DV_A] = o_c
    for (bb, h), s in zip(chains, ss):
        s_scr[bb, h] = s
        s_out_ref[bb, h] = s

    nw = prm_ref[2:3, :]
    for bb in range(nb):
        z = z_ref[bb]
        for h in range(H_A):
            oh = o_scr[bb, 0:tc, h * DV_A:(h + 1) * DV_A]
            zh = z[:, h * DV_A:(h + 1) * DV_A]
            oh = oh * lax.rsqrt(jnp.mean(oh * oh, axis=-1, keepdims=True) + RMS_EPS)
            o_ref[bb, :, h * DV_A:(h + 1) * DV_A] = oh * nw * _silu(zh)


def delta_mixer(x, h, wba, conv_a, prm, s0, db0, layer):
    b, l, _ = x.shape
    tc = min(256, l)
    cp = max(tc, CHUNK_A)
    nt = l // tc
    nb = min(b, max(1, PREP_GROUP // (H_A * (cp // CHUNK_A))) if l < CHUNK_A else 2)
    assert b % nb == 0
    kern = functools.partial(_delta_kernel, nb=nb, tc=tc, cp=cp)
    return pl.pallas_call(
        kern,
        grid=(b // nb, nt),
        in_specs=[pl.BlockSpec((nb, tc, D_MODEL), lambda i, t: (i, t, 0)),
                  pl.BlockSpec((nb, tc, QKVA_W), lambda i, t: (i, t, QKVA_OFF // QKVA_W)),
                  pl.BlockSpec((nb, tc, Z_W), lambda i, t: (i, t, Z_OFF // Z_W)),
                  pl.BlockSpec((None, D_MODEL, LANES), lambda i, t: (layer, 0, 0)),
                  pl.BlockSpec((None, CONV_A, W_QKV_A), lambda i, t: (layer, 0, 0)),
                  pl.BlockSpec((None, SUBLANES, LANES), lambda i, t: (layer, 0, 0)),
                  pl.BlockSpec((nb, H_A, DK_A, DV_A), lambda i, t: (i, 0, 0, 0)),
                  pl.BlockSpec((nb, HALO, W_QKV_A), lambda i, t: (i, 0, 0))],
        out_specs=[pl.BlockSpec((nb, tc, W_V_A), lambda i, t: (i, t, 0)),
                   pl.BlockSpec((nb, H_A, DK_A, DV_A), lambda i, t: (i, 0, 0, 0)),
                   pl.BlockSpec((nb, HALO, W_QKV_A), lambda i, t: (i, 0, 0))],
        out_shape=[jax.ShapeDtypeStruct((b, l, W_V_A), F32),
                   jax.ShapeDtypeStruct((b, H_A, DK_A, DV_A), F32),
                   jax.ShapeDtypeStruct((b, HALO, W_QKV_A), F32)],
        scratch_shapes=[pltpu.VMEM((nb, H_A, DK_A, DV_A), F32),
                        pltpu.VMEM((nb, tc + HALO, W_QKV_A), F32),
                        pltpu.VMEM((nb, cp, W_QK_A), F32),
                        pltpu.VMEM((nb, cp, W_QK_A), F32),
                        pltpu.VMEM((nb, cp, W_V_A), F32),
                        pltpu.VMEM((nb, cp, LANES), F32),
                        pltpu.VMEM((nb, cp, W_V_A), F32),
                        pltpu.VMEM((nb, cp, W_V_A), F32),
                        pltpu.VMEM((nb, 2 * cp, W_QK_A), BF16),
                        pltpu.VMEM((nb, cp, H_A * LANES), BF16),
                        pltpu.VMEM((nb, cp, W_QK_A), BF16),
                        pltpu.VMEM((nb, cp // CHUNK_A * SUBLANES, H_A * LANES), F32)],
        compiler_params=_cparams(("parallel", "arbitrary")),
        name="delta_mixer",
    )(x, h, h, wba, conv_a, prm, s0, db0)


def _sb_group(tiles, lm, bias, r0, chained, first_key_cols=1):
    zs, cs, tots = _sb_scores([(t[0], t[1], t[3]) for t in tiles], lm, bias, first_key_cols)
    return _sb_apply(zs, cs, tots, [t[2] for t in tiles], [t[3] for t in tiles], r0, chained)


def _sb_scores(tiles, lm, bias, first_key_cols):
    zs = [_dot_nt(q, kb) + bias for q, kb, _ in tiles]
    cs, tots = _sb_cumulate(zs, [t[2] for t in tiles], lm, first_key_cols)
    return zs, cs, tots


def _sb_cumulate(zs, valids, lm, first_key_cols):
    sps = [jnp.maximum(z, 0.0) + jnp.log(1.0 + jnp.exp(_neg_abs(z))) for z in zs]
    sps = [sp if v is None else jnp.where(v, sp, 0.0) for sp, v in zip(sps, valids)]
    later = [_dot(sp.astype(BF16), lm) for sp in sps]
    tots = [lt[:, 0:1] + jnp.sum(sp[:, 0:first_key_cols], axis=-1, keepdims=True) for lt, sp in zip(later, sps)]
    return [sp + lt for sp, lt in zip(sps, later)], tots


def _sb_apply(zs, cs, tots, vbs, valids, r0, chained):
    rs, ws = [], []
    r = r0
    for i, (z, c, tot, valid) in enumerate(zip(zs, cs, tots, valids)):
        r_prev = r if chained else r0[i]
        a = jnp.exp(z - c - r_prev)
        if valid is not None:
            a = jnp.where(valid, a, 0.0)
        ws.append(a.astype(BF16))
        r = r_prev + tot
        rs.append(r)
    return [_dot(a, vb) for a, vb in zip(ws, vbs)], rs


def _attn_prompt_kernel(bias_ref, q_ref, k_ref, v_ref, lm_ref, o_ref, kbf, vbf, acc, rsum, zbuf, cbuf, tbuf,
                        *, tq, tk, ts):
    h = pl.program_id(1)
    qi = pl.program_id(2)
    nsub = tq // ts
    ndiag = tq // tk

    @pl.when(qi == 0)
    def _():
        kbf[...] = k_ref[0].astype(BF16)
        vbf[...] = v_ref[0].astype(BF16)

    q = (q_ref[0] * (DH_B ** -0.5)).astype(BF16)
    qs = [q[s * ts:(s + 1) * ts] for s in range(nsub)]
    bias = bias_ref[h]
    lm = lm_ref[...]
    row = lax.broadcasted_iota(jnp.int32, (ts, tk), 0)
    col = lax.broadcasted_iota(jnp.int32, (ts, tk), 1)
    started = set()

    q0 = qi * tq

    def diag_subs(d):
        out = []
        for s in range(nsub):
            off = s * ts - d * tk
            if off + ts - 1 > 0:
                out.append((s, None if off >= tk else col < row + off))
        return out

    def diag_qk(d):
        kb = kbf[pl.ds(pl.multiple_of(q0 + d * tk, tk), tk), :]
        return [_dot_nt(qs[s], kb) + bias for s, _ in diag_subs(d)]

    def diag_finish(d, zs):
        cs, tots = _sb_cumulate(zs, [v for _, v in diag_subs(d)], lm, 1)
        return zs, cs, tots

    def diag_apply(d, scores):
        zs, cs, tots = scores
        subs = diag_subs(d)
        vb = vbf[pl.ds(pl.multiple_of(q0 + d * tk, tk), tk), :]
        r0 = [rsum[s * ts:(s + 1) * ts, :] if s in started else jnp.zeros((ts, 1), F32) for s, _ in subs]
        old = [acc[s * ts:(s + 1) * ts, :] if s in started else None for s, _ in subs]
        pvs, rs = _sb_apply(zs, cs, tots, [vb] * len(subs), [v for _, v in subs], r0, False)
        for (s, _), o, pv, r in zip(subs, old, pvs, rs):
            acc[s * ts:(s + 1) * ts, :] = pv if o is None else o + pv
            rsum[s * ts:(s + 1) * ts, :] = r
            started.add(s)

    scores = diag_finish(ndiag - 1, diag_qk(ndiag - 1))
    for d in reversed(range(ndiag)):
        z_next = diag_qk(d - 1) if d > 0 else None
        diag_apply(d, scores)
        if d > 0:
            scores = diag_finish(d - 1, z_next)

    n_past = qi * ndiag

    def past_block(i):
        return pl.multiple_of(jnp.maximum(q0 - (i + 1) * tk, 0), tk)

    rows = [slice(s * ts, (s + 1) * ts) for s in range(nsub)]

    def qk(i):
        kb = kbf[pl.ds(past_block(i), tk), :]
        return [_dot_nt(qs[s], kb) + bias for s in range(nsub)]

    def finish_scores(zs, slot):
        cs, tots = _sb_cumulate(zs, [None] * nsub, lm, 1)
        for r, z, c, t in zip(rows, zs, cs, tots):
            zbuf[slot, r, :] = z
            cbuf[slot, r, :] = c
            tbuf[slot, r, :] = t

    def apply(i, slot):
        vb = vbf[pl.ds(past_block(i), tk), :]
        pvs, rs = _sb_apply([zbuf[slot, r, :] for r in rows], [cbuf[slot, r, :] for r in rows],
                            [tbuf[slot, r, :] for r in rows], [vb] * nsub, [None] * nsub,
                            [rsum[r, :] for r in rows], False)
        for r, pv, rn in zip(rows, pvs, rs):
            acc[r, :] += pv
            rsum[r, :] = rn

    @pl.when(n_past > 0)
    def _():
        finish_scores(qk(0), 0)

    def body(j, carry):
        for slot in range(2):
            i = 2 * j + slot
            z_next = qk(i + 1)
            apply(i, slot)
            finish_scores(z_next, 1 - slot)
        return carry

    lax.fori_loop(0, n_past // 2, body, 0)
    o_ref[0] = acc[...]


def _later_ones(n):
    r = lax.broadcasted_iota(jnp.int32, (n, n), 0)
    c = lax.broadcasted_iota(jnp.int32, (n, n), 1)
    return (r > c).astype(BF16)


def attn_prompt(h, sb_bias):
    b, l, _ = h.shape
    tq = min(1024, l)
    tk = min(256, tq)
    ts = min(256, tq)
    assert l == tq or (tq // tk) % 2 == 0
    qb = QKVB_OFF // DH_B
    kern = functools.partial(_attn_prompt_kernel, tq=tq, tk=tk, ts=ts)
    grid_spec = pltpu.PrefetchScalarGridSpec(
        num_scalar_prefetch=0,
        grid=(b, H_B, l // tq),
        in_specs=[pl.BlockSpec(memory_space=pltpu.SMEM),
                  pl.BlockSpec((1, tq, DH_B), lambda i, hh, j: (i, j, qb + hh)),
                  pl.BlockSpec((1, l, DH_B), lambda i, hh, j: (i, 0, qb + H_B + hh)),
                  pl.BlockSpec((1, l, DH_B), lambda i, hh, j: (i, 0, qb + 2 * H_B + hh)),
                  pl.BlockSpec((tk, tk), lambda i, hh, j: (0, 0))],
        out_specs=pl.BlockSpec((1, tq, DH_B), lambda i, hh, j: (i, j, hh)),
        scratch_shapes=[pltpu.VMEM((l, DH_B), BF16),
                        pltpu.VMEM((l, DH_B), BF16),
                        pltpu.VMEM((tq, DH_B), F32),
                        pltpu.VMEM((tq, 1), F32),
                        pltpu.VMEM((2, tq, tk), F32),
                        pltpu.VMEM((2, tq, tk), F32),
                        pltpu.VMEM((2, tq, 1), F32)])
    return pl.pallas_call(
        kern,
        grid_spec=grid_spec,
        out_shape=jax.ShapeDtypeStruct((b, l, W_B), F32),
        compiler_params=_cparams(("parallel", "parallel", "arbitrary")),
        name="attn_prompt",
    )(sb_bias, h, h, h, _later_ones(tk))


def _attn_sample_kernel(pt_ref, bias_ref, qkv_ref, *rest, seq, pages_per_step, page_group, n_steps):
    g_pages = pages_per_step
    k_refs = rest[:g_pages]
    v_refs = rest[g_pages:2 * g_pages]
    lm_ref = rest[2 * g_pages]
    lx_ref = rest[2 * g_pages + 1]
    o_ref = rest[2 * g_pages + 2]
    qall, acc, rsum = rest[2 * g_pages + 3:]
    s = pl.program_id(1)
    rows = H_B * seq
    pcols = PAGE_SIZE * H_B
    rid = lax.broadcasted_iota(jnp.int32, (rows, 1), 0)
    bias = jnp.zeros((rows, 1), F32)
    for hh in range(H_B):
        bias = jnp.where((rid >= hh * seq) & (rid < (hh + 1) * seq), bias_ref[hh], bias)

    @pl.when(s == 0)
    def _():
        qkv = qkv_ref[0]
        q = qkv[:, 0:W_B] * (DH_B ** -0.5)
        qall[...] = jnp.concatenate([q[:, hh * DH_B:(hh + 1) * DH_B] for hh in range(H_B)], axis=0).astype(BF16)
        lane = lax.broadcasted_iota(jnp.int32, (seq, W_B), 1)
        parts = [jnp.where((lane >= hh * DH_B) & (lane < (hh + 1) * DH_B), q, 0.0) for hh in range(H_B)]
        qbd = jnp.concatenate(parts, axis=0).astype(BF16)
        zrows = jnp.zeros((PAGE_SIZE - seq, W_B), F32)
        k_own = jnp.concatenate([qkv[:, W_B:2 * W_B], zrows], axis=0).astype(BF16)
        v_own = jnp.concatenate([qkv[:, 2 * W_B:3 * W_B], zrows], axis=0).astype(BF16)
        key = lax.broadcasted_iota(jnp.int32, (rows, PAGE_SIZE), 1)
        qpos = lax.broadcasted_iota(jnp.int32, (rows, PAGE_SIZE), 0) % seq
        pvs, rs = _sb_group([(qbd, k_own, v_own, key < qpos)], lm_ref[...], bias, jnp.zeros((rows, 1), F32), True)
        acc[...] = jnp.concatenate(
            [pvs[0][hh * seq:(hh + 1) * seq, hh * DH_B:(hh + 1) * DH_B] for hh in range(H_B)], axis=0)
        rsum[...] = rs[0]

    own = (lax.broadcasted_iota(jnp.int32, (rows, pcols), 1) % H_B
           == lax.broadcasted_iota(jnp.int32, (rows, pcols), 0) // seq)
    qa = qall[...]
    lx = lx_ref[...]
    r = rsum[...]
    total = acc[...]
    for g0 in reversed(range(0, g_pages, page_group)):
        tiles = [(qa, k_refs[gi][0, 0].astype(BF16), v_refs[gi][0, 0].astype(BF16), own)
                 for gi in reversed(range(g0, g0 + page_group))]
        pvs, rs = _sb_group(tiles, lx, bias, r, True, H_B)
        r = rs[-1]
        for pv in pvs:
            total = total + pv
    acc[...] = total
    rsum[...] = r

    @pl.when(s == n_steps - 1)
    def _():
        a = acc[...]
        o_ref[0] = jnp.concatenate([a[hh * seq:(hh + 1) * seq, :] for hh in range(H_B)], axis=1)


def attn_sample(h, cache_k, cache_v, page_table, sb_bias, layer):
    b, seq, _ = h.shape
    depth, n_pool = cache_k.shape[:2]
    n_pages = page_table.shape[0] // b
    g_pages = min(32, n_pages)
    n_steps = n_pages // g_pages
    rows = H_B * seq
    pcols = PAGE_SIZE * H_B
    ck = cache_k.reshape(depth, n_pool, pcols, DH_B)
    cv = cache_v.reshape(depth, n_pool, pcols, DH_B)
    kidx = lax.broadcasted_iota(jnp.int32, (pcols, pcols), 0) // H_B
    lexp = (kidx > kidx.T).astype(BF16)

    def page_map(gi):
        def index_map(i, s, pt):
            return (layer, pt[i * n_pages + (n_steps - 1 - s) * g_pages + gi], 0, 0)
        return index_map

    page_specs = [pl.BlockSpec((1, 1, pcols, DH_B), page_map(gi)) for gi in range(g_pages)]
    kern = functools.partial(_attn_sample_kernel, seq=seq, pages_per_step=g_pages, page_group=min(4, g_pages),
                             n_steps=n_steps)
    grid_spec = pltpu.PrefetchScalarGridSpec(
        num_scalar_prefetch=1,
        grid=(b, n_steps),
        in_specs=([pl.BlockSpec(memory_space=pltpu.SMEM),
                   pl.BlockSpec((1, seq, QKVB_W), lambda i, s, pt: (i, 0, QKVB_OFF // QKVB_W))]
                  + page_specs + page_specs
                  + [pl.BlockSpec((PAGE_SIZE, PAGE_SIZE), lambda i, s, pt: (0, 0)),
                     pl.BlockSpec((pcols, pcols), lambda i, s, pt: (0, 0))]),
        out_specs=pl.BlockSpec((1, seq, W_B), lambda i, s, pt: (i, 0, 0)),
        scratch_shapes=[pltpu.VMEM((rows, DH_B), BF16),
                        pltpu.VMEM((rows, DH_B), F32),
                        pltpu.VMEM((rows, 1), F32)])
    return pl.pallas_call(
        kern,
        grid_spec=grid_spec,
        out_shape=jax.ShapeDtypeStruct((b, seq, W_B), F32),
        compiler_params=_cparams(("parallel", "arbitrary")),
        name="attn_sample",
    )(page_table, sb_bias, h, *([ck] * g_pages), *([cv] * g_pages), _later_ones(PAGE_SIZE), lexp)


def _merge_kernel(x_ref, g_ref, c_ref, oa_ref, ob_ref, wpa_ref, wpb_ref, wpc_ref, wo_ref, conv_ref, ln_ref, sb0_ref,
                  o_ref, tail_ref, ext_scr, *, tm):
    t = pl.program_id(1)

    @pl.when(t == 0)
    def _():
        ext_scr[0:HALO, :] = sb0_ref[0]

    cc = c_ref[0]
    u = cc[:, 2 * W_C:] * cc[:, :W_C]
    ext_scr[HALO:HALO + tm, :] = u
    w = conv_ref[...]
    y = ext_scr[HALO - 2:HALO - 2 + tm, :] * w[0:1, :] + ext_scr[HALO - 1:HALO - 1 + tm, :] * w[1:2, :] + u * w[2:3, :]
    tail = ext_scr[tm:tm + HALO, :]
    ext_scr[0:HALO, :] = tail
    tail_ref[0] = tail
    o_c = cc[:, W_C:2 * W_C] * y

    g = g_ref[0]
    merged = (_sigmoid(g[:, :D_MODEL]) * _dot(oa_ref[0].astype(BF16), wpa_ref[...])
              + _sigmoid(g[:, D_MODEL:2 * D_MODEL]) * _dot(ob_ref[0].astype(BF16), wpb_ref[...])
              + _sigmoid(g[:, 2 * D_MODEL:]) * _dot(o_c.astype(BF16), wpc_ref[...]))
    r = DEEPNORM_ALPHA * x_ref[0] + _dot(merged.astype(BF16), wo_ref[...])
    o_ref[0] = _layer_norm(r, ln_ref[0:1, :], ln_ref[1:2, :])


def merge(x, h, o_a, o_b, wpa, wpb, wpc, wo, conv_c, ln, sb0, layer):
    b, l, _ = x.shape
    tm = min(512, l)
    kern = functools.partial(_merge_kernel, tm=tm)
    const = lambda i, t: (layer, 0, 0)
    return pl.pallas_call(
        kern,
        grid=(b, l // tm),
        in_specs=[pl.BlockSpec((1, tm, D_MODEL), lambda i, t: (i, t, 0)),
                  pl.BlockSpec((1, tm, G_W), lambda i, t: (i, t, G_OFF // G_W)),
                  pl.BlockSpec((1, tm, C_W), lambda i, t: (i, t, C_OFF // C_W)),
                  pl.BlockSpec((1, tm, W_V_A), lambda i, t: (i, t, 0)),
                  pl.BlockSpec((1, tm, W_B), lambda i, t: (i, t, 0)),
                  pl.BlockSpec((None, W_V_A, D_MODEL), const),
                  pl.BlockSpec((None, W_B, D_MODEL), const),
                  pl.BlockSpec((None, W_C, D_MODEL), const),
                  pl.BlockSpec((None, D_MODEL, D_MODEL), const),
                  pl.BlockSpec((None, CONV_C, W_C), const),
                  pl.BlockSpec((None, 2, D_MODEL), const),
                  pl.BlockSpec((1, HALO, W_C), lambda i, t: (i, 0, 0))],
        out_specs=[pl.BlockSpec((1, tm, D_MODEL), lambda i, t: (i, t, 0)),
                   pl.BlockSpec((1, HALO, W_C), lambda i, t: (i, 0, 0))],
        out_shape=[jax.ShapeDtypeStruct((b, l, D_MODEL), F32),
                   jax.ShapeDtypeStruct((b, HALO, W_C), F32)],
        scratch_shapes=[pltpu.VMEM((tm + HALO, W_C), F32)],
        compiler_params=_cparams(("parallel", "arbitrary")),
        name="merge",
    )(x, h, h, o_a, o_b, wpa, wpb, wpc, wo, conv_c, ln, sb0)


def _moe_kernel(x_ref, wr_ref, br_ref, wgu_ref, wdn_ref, ln_ref, o_ref, xb, gates, acc, *, tm, eps):
    e = pl.program_id(1)
    lane = lax.broadcasted_iota(jnp.int32, (tm, LANES), 1)

    @pl.when(e == 0)
    def _():
        x = x_ref[...]
        xh, xl = _split2(x)
        wh, wl = _split2(wr_ref[...])
        logits = _dot(xh, wh) + _dot(xh, wl) + _dot(xl, wh) + br_ref[...]
        lanef = lane.astype(F32)
        big = float(LANES)
        is_g = lane < N_GROUPS
        gl = jnp.where(is_g, logits, -jnp.inf)
        gmax = jnp.max(gl, axis=-1, keepdims=True)
        gsel = jnp.min(jnp.where(gl == gmax, lanef, big), axis=-1, keepdims=True)
        pg_sel = 1.0 / jnp.sum(jnp.where(is_g, jnp.exp(gl - gmax), 0.0), axis=-1, keepdims=True)
        lo = N_GROUPS + gsel * EXPERTS_PER_GROUP
        ev = jnp.where((lanef >= lo) & (lanef < lo + EXPERTS_PER_GROUP), logits, -jnp.inf)
        v1 = jnp.max(ev, axis=-1, keepdims=True)
        i1 = jnp.min(jnp.where(ev == v1, lanef, big), axis=-1, keepdims=True)
        ev2 = jnp.where(lanef == i1, -jnp.inf, ev)
        v2 = jnp.max(ev2, axis=-1, keepdims=True)
        i2 = jnp.min(jnp.where(ev2 == v2, lanef, big), axis=-1, keepdims=True)
        e2 = jnp.exp(v2 - v1)
        den = 1.0 + e2
        gates[...] = jnp.where(lanef == i1, pg_sel / den, jnp.where(lanef == i2, pg_sel * e2 / den, 0.0))
        xb[...] = xh
        acc[...] = jnp.zeros((tm, D_MODEL), F32)

    xv = xb[...]
    g_all = gates[...]
    hhs = [_dot(xv, wgu_ref[j]) for j in range(eps)]
    acts = []
    for j, hh in enumerate(hhs):
        ge = jnp.sum(jnp.where(lane == N_GROUPS + e * eps + j, g_all, 0.0), axis=-1, keepdims=True)
        acts.append((_silu(hh[:, :D_EXPERT]) * hh[:, D_EXPERT:] * ge).astype(BF16))
    acc[...] += _dot(jnp.concatenate(acts, axis=1), wdn_ref[...].reshape(eps * D_EXPERT, D_MODEL))

    @pl.when(e == N_EXPERTS // eps - 1)
    def _():
        r = DEEPNORM_ALPHA * x_ref[...] + acc[...]
        o_ref[...] = _layer_norm(r, ln_ref[0:1, :], ln_ref[1:2, :])


def moe(x2d, wr, br, wgu, wdn, ln, layer):
    t_rows = x2d.shape[0]
    tm = min(1024, t_rows)
    eps = 4
    kern = functools.partial(_moe_kernel, tm=tm, eps=eps)
    return pl.pallas_call(
        kern,
        grid=(t_rows // tm, N_EXPERTS // eps),
        in_specs=[pl.BlockSpec((tm, D_MODEL), lambda i, e: (i, 0)),
                  pl.BlockSpec((None, D_MODEL, LANES), lambda i, e: (layer, 0, 0)),
                  pl.BlockSpec((None, 1, LANES), lambda i, e: (layer, 0, 0)),
                  pl.BlockSpec((eps, D_MODEL, 2 * D_EXPERT), lambda i, e: (layer * (N_EXPERTS // eps) + e, 0, 0)),
                  pl.BlockSpec((eps, D_EXPERT, D_MODEL), lambda i, e: (layer * (N_EXPERTS // eps) + e, 0, 0)),
                  pl.BlockSpec((None, 2, D_MODEL), lambda i, e: (layer, 0, 0))],
        out_specs=pl.BlockSpec((tm, D_MODEL), lambda i, e: (i, 0)),
        out_shape=jax.ShapeDtypeStruct((t_rows, D_MODEL), F32),
        scratch_shapes=[pltpu.VMEM((tm, D_MODEL), BF16),
                        pltpu.VMEM((tm, LANES), F32),
                        pltpu.VMEM((tm, D_MODEL), F32)],
        compiler_params=_cparams(("parallel", "arbitrary")),
        name="moe",
    )(x2d, wr, br, wgu, wdn, ln)


def _prep_params(w_in, conv_a, a_log, dt_bias, norm_a, conv_c, w_pa, w_pb, w_pc, w_o, ln_g, ln_b,
                 w_rg, b_rg, w_re, b_re, w_gu, w_down):
    depth = w_in.shape[0]
    o_z = W_QKV_A
    o_ba = o_z + W_V_A
    o_qb = o_ba + 2 * H_A
    o_c = o_qb + 3 * W_B
    o_g = o_c + 3 * W_C
    w_main = jnp.concatenate([w_in[:, :, o_g:], w_in[:, :, :W_QKV_A], w_in[:, :, o_c:o_g], w_in[:, :, o_qb:o_c],
                              w_in[:, :, o_z:o_ba]], axis=2).astype(BF16)
    w_ba = jnp.pad(w_in[:, :, o_ba:o_qb], ((0, 0), (0, 0), (0, LANES - 2 * H_A))).astype(BF16)
    prm = jnp.zeros((depth, SUBLANES, LANES), F32)
    prm = prm.at[:, 0, H_A:2 * H_A].set(a_log).at[:, 1, H_A:2 * H_A].set(dt_bias).at[:, 2, :DV_A].set(norm_a)
    pad_r = LANES - N_GROUPS - N_EXPERTS
    w_r = jnp.pad(jnp.concatenate([w_rg, w_re], axis=2), ((0, 0), (0, 0), (0, pad_r)))
    b_r = jnp.pad(jnp.concatenate([b_rg, b_re], axis=1), ((0, 0), (0, pad_r))).reshape(depth, 1, LANES)
    return dict(
        w_main=w_main, w_ba=w_ba, conv_a=conv_a, prm=prm, conv_c=conv_c,
        w_pa=w_pa.astype(BF16), w_pb=w_pb.astype(BF16), w_pc=w_pc.astype(BF16), w_o=w_o.astype(BF16),
        ln0=jnp.stack([ln_g[:, 0], ln_b[:, 0]], axis=1), ln1=jnp.stack([ln_g[:, 1], ln_b[:, 1]], axis=1),
        w_r=w_r, b_r=b_r,
        w_gu=w_gu.astype(BF16).reshape(depth * N_EXPERTS, D_MODEL, 2 * D_EXPERT),
        w_down=w_down.astype(BF16).reshape(depth * N_EXPERTS, D_EXPERT, D_MODEL))


def _pad_tail(buf):
    return jnp.pad(buf, ((0, 0), (HALO - buf.shape[1], 0), (0, 0)))


def _layer(x, p, s0, db0, sb0, attn_fn, layer, depth, k_buf, v_buf):
    b, l, _ = x.shape
    h, k_buf, v_buf = in_proj(x.reshape(b * l, D_MODEL), p["w_main"], layer, depth, k_buf, v_buf)
    h = h.reshape(b, l, H_COLS)
    o_a, s_new, dtail = delta_mixer(x, h, p["w_ba"], p["conv_a"], p["prm"], s0, _pad_tail(db0), layer)
    o_b = attn_fn(h)
    x1, stail = merge(x, h, o_a, o_b, p["w_pa"], p["w_pb"], p["w_pc"], p["w_o"], p["conv_c"], p["ln0"],
                      _pad_tail(sb0), layer)
    x2 = moe(x1.reshape(b * l, D_MODEL), p["w_r"], p["b_r"], p["w_gu"], p["w_down"], p["ln1"], layer)
    x2 = x2.reshape(b, l, D_MODEL)
    return (x2, s_new, dtail[:, HALO - (CONV_A - 1):], stail[:, HALO - (CONV_C - 1):]), k_buf, v_buf


def kernel(x_prompt, x_sample, cache_k, cache_v, state_delta, state_dconv, state_sconv, page_table, w_in, conv_a, a_log, dt_bias, norm_a, sb_bias, conv_c, w_pa, w_pb, w_pc, w_o, ln_g, ln_b, w_rg, b_rg, w_re, b_re, w_gu, w_down):
    bp, lp, _ = x_prompt.shape
    bs, ls, _ = x_sample.shape
    depth = w_in.shape[0]
    pt = page_table.reshape(-1).astype(jnp.int32)
    xp, xs = x_prompt, x_sample
    outs_p = [[] for _ in range(3)]
    outs_s = [[] for _ in range(3)]
    pk = pv = sk = sv = None
    p = _prep_params(w_in, conv_a, a_log, dt_bias, norm_a, conv_c, w_pa, w_pb, w_pc, w_o, ln_g, ln_b,
                     w_rg, b_rg, w_re, b_re, w_gu, w_down)
    for l in range(depth):
        bias = sb_bias[l]
        res, pk, pv = _layer(xp, p,
                             jnp.zeros((bp, H_A, DK_A, DV_A), F32),
                             jnp.zeros((bp, CONV_A - 1, W_QKV_A), F32),
                             jnp.zeros((bp, CONV_C - 1, W_C), F32),
                             lambda h: attn_prompt(h, bias), l, depth, pk, pv)
        xp = res[0]
        for acc_list, r in zip(outs_p, res[1:]):
            acc_list.append(r)
        res, sk, sv = _layer(xs, p, state_delta[l], state_dconv[l], state_sconv[l],
                             lambda h: attn_sample(h, cache_k, cache_v, pt, bias, l), l, depth, sk, sv)
        xs = res[0]
        for acc_list, r in zip(outs_s, res[1:]):
            acc_list.append(r)
    return (xp, xs, *[jnp.stack(o) for o in outs_p],
            pk.reshape(depth, bp, lp, H_B, DH_B), pv.reshape(depth, bp, lp, H_B, DH_B),
            *[jnp.stack(o) for o in outs_s],
            sk.reshape(depth, bs, ls, H_B, DH_B), sv.reshape(depth, bs, ls, H_B, DH_B))
```

```python
import functools

import jax
import jax.numpy as jnp
from jax import lax
from jax.experimental import pallas as pl
from jax.experimental.pallas import tpu as pltpu

F32 = jnp.float32
BF16 = jnp.bfloat16

D_MODEL = 1024
DEPTH = 4
H_A = 4
DK_A = 128
DV_A = 128
CONV_A = 4
CHUNK_A = 64
H_B = 4
DH_B = 128
W_C = 512
CONV_C = 3
N_GROUPS = 4
EXPERTS_PER_GROUP = 4
N_EXPERTS = N_GROUPS * EXPERTS_PER_GROUP
D_EXPERT = 256
PAGE_SIZE = 128

W_QK_A = H_A * DK_A
W_V_A = H_A * DV_A
W_QKV_A = 2 * W_QK_A + W_V_A
W_B = H_B * DH_B

DEEPNORM_ALPHA = (2.0 * DEPTH) ** 0.25
LN_EPS = 1e-5
RMS_EPS = 1e-6

G_OFF, G_W = 0, 3 * D_MODEL
QKVA_OFF, QKVA_W = 3072, W_QKV_A
C_OFF, C_W = 4608, 3 * W_C
QKVB_OFF, QKVB_W = 6144, 3 * W_B
Z_OFF, Z_W = 7680, W_V_A
H_COLS = 8192
LANES = 128
SUBLANES = 8
HALO = SUBLANES
PREP_GROUP = 32

VMEM_LIMIT = 56 * 1024 * 1024


def _cparams(sem):
    return pltpu.CompilerParams(dimension_semantics=sem, vmem_limit_bytes=VMEM_LIMIT)


def _dot(a, b):
    return jnp.dot(a, b, preferred_element_type=F32)


def _dot_nt(a, b):
    return lax.dot_general(a, b, (((1,), (1,)), ((), ())), preferred_element_type=F32)


def _dot_tn(a, b):
    return lax.dot_general(a, b, (((0,), (0,)), ((), ())), preferred_element_type=F32)


def _split2(x):
    hi = x.astype(BF16)
    lo = (x - hi.astype(F32)).astype(BF16)
    return hi, lo


def _split3(x):
    hi = x.astype(BF16)
    r = x - hi.astype(F32)
    mid = r.astype(BF16)
    lo = (r - mid.astype(F32)).astype(BF16)
    return hi, mid, lo


def _mm_sel(sel, x):
    hi, mid, lo = _split3(x)
    return _dot(sel, hi) + _dot(sel, mid) + _dot(sel, lo)


def _neg_abs(x):
    bits = lax.bitcast_convert_type(x, jnp.uint32) | jnp.uint32(0x80000000)
    return lax.bitcast_convert_type(bits, F32)


def _softplus(x):
    return jnp.maximum(x, 0.0) + jnp.log1p(jnp.exp(-jnp.abs(x)))


def _sigmoid(x):
    return 1.0 / (1.0 + jnp.exp(-x))


def _silu(x):
    return x * _sigmoid(x)


def _layer_norm(r, g, b):
    mu = jnp.mean(r, axis=-1, keepdims=True)
    d = r - mu
    var = jnp.mean(d * d, axis=-1, keepdims=True)
    return d * lax.rsqrt(var + LN_EPS) * g + b


K_COL = QKVB_OFF + W_B
V_COL = QKVB_OFF + 2 * W_B


def _inproj_kernel(x_ref, w_ref, *refs, tm, tn):
    o_ref, k_ref, v_ref = refs[-3:]
    j = pl.program_id(1)
    o_ref[...] = _dot(x_ref[...].astype(BF16), w_ref[...])

    def rows_out(ref, col):
        @pl.when(j == col // tn)
        def _():
            for hh in range(H_B):
                c0 = col % tn + hh * DH_B
                ref[0, pl.ds(hh, tm, stride=H_B), :] = o_ref[:, c0:c0 + DH_B]

    rows_out(k_ref, K_COL)
    rows_out(v_ref, V_COL)


def in_proj(x2d, w, layer, depth, k_buf, v_buf):
    t_rows = x2d.shape[0]
    tm = min(1024, t_rows)
    tn = 2048
    assert K_COL % tn + W_B <= tn and V_COL % tn + W_B <= tn
    rows_shape = jax.ShapeDtypeStruct((depth, t_rows * H_B, DH_B), F32)
    rows_spec = pl.BlockSpec((1, tm * H_B, DH_B), lambda i, j: (layer, i, 0))
    carried = [] if k_buf is None else [k_buf, v_buf]
    return pl.pallas_call(
        functools.partial(_inproj_kernel, tm=tm, tn=tn),
        grid=(t_rows // tm, H_COLS // tn),
        in_specs=[pl.BlockSpec((tm, D_MODEL), lambda i, j: (i, 0)),
                  pl.BlockSpec((None, D_MODEL, tn), lambda i, j: (layer, 0, j))]
                 + [pl.BlockSpec(memory_space=pl.ANY)] * len(carried),
        out_specs=[pl.BlockSpec((tm, tn), lambda i, j: (i, j)), rows_spec, rows_spec],
        out_shape=[jax.ShapeDtypeStruct((t_rows, H_COLS), F32), rows_shape, rows_shape],
        input_output_aliases={2: 1, 3: 2} if carried else {},
        compiler_params=_cparams(("parallel", "arbitrary")),
        name="in_proj",
    )(x2d, w, *carried)


def _delta_kernel(x_ref, qkv_ref, z_ref, wba_ref, conv_ref, prm_ref, s0_ref, db0_ref,
                  o_ref, s_out_ref, tail_ref,
                  s_scr, ext_scr, q_scr, k_scr, v_scr, bg_scr, o_scr, u_scr, wq_scr, qk_scr, kd_scr, gl_scr,
                  *, nb, tc, cp):
    c = CHUNK_A
    t = pl.program_id(1)

    @pl.when(t == 0)
    def _():
        s_scr[...] = s0_ref[...]
        ext_scr[:, 0:HALO, :] = db0_ref[...]

    w = conv_ref[...]
    for bb in range(nb):
        u = qkv_ref[bb]
        ext_scr[bb, HALO:HALO + tc, :] = u
        y = (ext_scr[bb, HALO - 3:HALO - 3 + tc, :] * w[0:1, :] + ext_scr[bb, HALO - 2:HALO - 2 + tc, :] * w[1:2, :]
             + ext_scr[bb, HALO - 1:HALO - 1 + tc, :] * w[2:3, :] + u * w[3:4, :])
        tail = ext_scr[bb, tc:tc + HALO, :]
        ext_scr[bb, 0:HALO, :] = tail
        tail_ref[bb] = tail
        y = _silu(y)

        ba = _dot(x_ref[bb].astype(BF16), wba_ref[...])
        beta = _sigmoid(ba)
        g = -jnp.exp(prm_ref[0:1, :]) * _softplus(ba + prm_ref[1:2, :])
        lane = lax.broadcasted_iota(jnp.int32, (tc, LANES), 1)
        bg = jnp.where(lane < H_A, beta, g)

        if cp > tc:
            q_scr[bb, tc:cp, :] = jnp.zeros((cp - tc, W_QK_A), F32)
            k_scr[bb, tc:cp, :] = jnp.zeros((cp - tc, W_QK_A), F32)
            v_scr[bb, tc:cp, :] = jnp.zeros((cp - tc, W_V_A), F32)
            bg_scr[bb, tc:cp, :] = jnp.zeros((cp - tc, LANES), F32)
        bg_scr[bb, 0:tc, :] = bg
        for h in range(H_A):
            qh = y[:, h * DK_A:(h + 1) * DK_A]
            kh = y[:, W_QK_A + h * DK_A:W_QK_A + (h + 1) * DK_A]
            qn = qh * lax.rsqrt(jnp.sum(qh * qh, axis=-1, keepdims=True) + RMS_EPS) * (DK_A ** -0.5)
            kn = kh * lax.rsqrt(jnp.sum(kh * kh, axis=-1, keepdims=True) + RMS_EPS)
            q_scr[bb, 0:tc, h * DK_A:(h + 1) * DK_A] = qn
            k_scr[bb, 0:tc, h * DK_A:(h + 1) * DK_A] = kn
        v_scr[bb, 0:tc, :] = y[:, 2 * W_QK_A:]

    ri = lax.broadcasted_iota(jnp.int32, (c, c), 0)
    ci = lax.broadcasted_iota(jnp.int32, (c, c), 1)
    tril = (ri >= ci).astype(BF16)
    triu_f = (ri <= ci).astype(F32)
    ones = jnp.ones((c, c), BF16)

    def prep_many(ins):
        qcs, kcs, vcs, betas, gs = zip(*ins)
        gbs = [jnp.broadcast_to(g_c, (c, LANES)) for g_c in gs]
        gcs = [_mm_sel(tril, gb) for gb in gbs]
        grows = [_mm_sel(ones, gb[:, :c] * triu_f) for gb in gbs]
        decs = [jnp.exp(jnp.where(ri >= ci, gc[:, :c] - gr, -jnp.inf)) for gc, gr in zip(gcs, grows)]
        kbs = [kc * b for kc, b in zip(kcs, betas)]
        kqs = [_dot_nt(jnp.concatenate([kb, qc], axis=0).astype(BF16), kc.astype(BF16))
               for kb, qc, kc in zip(kbs, qcs, kcs)]
        qks = [kq[c:] * dec for kq, dec in zip(kqs, decs)]
        egcs = [jnp.exp(gc) for gc in gcs]
        rhss = [jnp.concatenate([vc * b, kb * egc], axis=1) for vc, b, kb, egc in zip(vcs, betas, kbs, egcs)]
        ns = [-jnp.where(ri > ci, kq[:c] * dec, 0.0) for kq, dec in zip(kqs, decs)]
        yys = ns
        for _ in range(5):
            nbs = [n.astype(BF16) for n in ns]
            ns = [_dot(nb, nb) for nb in nbs]
            prods = [_dot(yy.astype(BF16), n.astype(BF16)) for yy, n in zip(yys, ns)]
            yys = [yy + n + p for yy, n, p in zip(yys, ns, prods)]
        sols = [rhs + _dot(yy.astype(BF16), rhs.astype(BF16)) for rhs, yy in zip(rhss, yys)]
        gls = [gc[c - 1:c, :] for gc in gcs]
        outs = []
        for sol, qc, kc, qk, egc, gc, gl in zip(sols, qcs, kcs, qks, egcs, gcs, gls):
            wq = jnp.concatenate([sol[:, DV_A:], qc * egc], axis=0).astype(BF16)
            k_dec = (kc * jnp.exp(gl - gc)).astype(BF16)
            outs.append((sol[:, :DV_A], wq, qk.astype(BF16), k_dec,
                         jnp.broadcast_to(jnp.exp(gl), (SUBLANES, LANES))))
        return outs

    n_chunks = cp // c
    problems = [(bb, ic, h) for bb in range(nb) for ic in range(n_chunks) for h in range(H_A)]
    for p0 in range(0, len(problems), PREP_GROUP):
        group = problems[p0:p0 + PREP_GROUP]
        ins = []
        for bb, ic, h in group:
            rows = slice(ic * c, (ic + 1) * c)
            ins.append((q_scr[bb, rows, h * DK_A:(h + 1) * DK_A], k_scr[bb, rows, h * DK_A:(h + 1) * DK_A],
                        v_scr[bb, rows, h * DV_A:(h + 1) * DV_A], bg_scr[bb, rows, h:h + 1],
                        bg_scr[bb, rows, H_A + h:H_A + h + 1]))
        for (bb, ic, h), (uu, wq, qk, k_dec, egl) in zip(group, prep_many(ins)):
            rows = slice(ic * c, (ic + 1) * c)
            u_scr[bb, rows, h * DV_A:(h + 1) * DV_A] = uu
            wq_scr[bb, 2 * ic * c:2 * (ic + 1) * c, h * DK_A:(h + 1) * DK_A] = wq
            qk_scr[bb, rows, h * LANES:h * LANES + c] = qk
            kd_scr[bb, rows, h * DK_A:(h + 1) * DK_A] = k_dec
            gl_scr[bb, ic * SUBLANES:(ic + 1) * SUBLANES, h * LANES:(h + 1) * LANES] = egl

    chains = [(bb, h) for bb in range(nb) for h in range(H_A)]
    ss = [s_scr[bb, h] for bb, h in chains]
    for ic in range(n_chunks):
        rows = slice(ic * c, (ic + 1) * c)
        us = [u_scr[bb, rows, h * DV_A:(h + 1) * DV_A] for bb, h in chains]
        wqs = [wq_scr[bb, 2 * ic * c:2 * (ic + 1) * c, h * DK_A:(h + 1) * DK_A] for bb, h in chains]
        qks = [qk_scr[bb, rows, h * LANES:h * LANES + c] for bb, h in chains]
        kds = [kd_scr[bb, rows, h * DK_A:(h + 1) * DK_A] for bb, h in chains]
        egls = [gl_scr[bb, ic * SUBLANES:ic * SUBLANES + 1, h * LANES:(h + 1) * LANES] for bb, h in chains]
        wss = [_dot(wq, s.astype(BF16)) for wq, s in zip(wqs, ss)]
        vns = [(uu - ws[:c]).astype(BF16) for uu, ws in zip(us, wss)]
        ocs = [ws[c:] + _dot(qk, vn) for ws, qk, vn in zip(wss, qks, vns)]
        ss = [s * egl + _dot_tn(kd, vn) for s, egl, kd, vn in zip(ss, egls, kds, vns)]
        for (bb, h), o_c in zip(chains, ocs):
            o_scr[bb, rows, h * DV_A:(h + 1) * DV_A] = o_c
    for (bb, h), s in zip(chains, ss):
        s_scr[bb, h] = s
        s_out_ref[bb, h] = s

    nw = prm_ref[2:3, :]
    for bb in range(nb):
        z = z_ref[bb]
        for h in range(H_A):
            oh = o_scr[bb, 0:tc, h * DV_A:(h + 1) * DV_A]
            zh = z[:, h * DV_A:(h + 1) * DV_A]
            oh = oh * lax.rsqrt(jnp.mean(oh * oh, axis=-1, keepdims=True) + RMS_EPS)
            o_ref[bb, :, h * DV_A:(h + 1) * DV_A] = oh * nw * _silu(zh)


def delta_mixer(x, h, wba, conv_a, prm, s0, db0, layer):
    b, l, _ = x.shape
    tc = min(256, l)
    cp = max(tc, CHUNK_A)
    nt = l // tc
    nb = min(b, max(1, PREP_GROUP // (H_A * (cp // CHUNK_A))) if l < CHUNK_A else 2)
    assert b % nb == 0
    kern = functools.partial(_delta_kernel, nb=nb, tc=tc, cp=cp)
    return pl.pallas_call(
        kern,
        grid=(b // nb, nt),
        in_specs=[pl.BlockSpec((nb, tc, D_MODEL), lambda i, t: (i, t, 0)),
                  pl.BlockSpec((nb, tc, QKVA_W), lambda i, t: (i, t, QKVA_OFF // QKVA_W)),
                  pl.BlockSpec((nb, tc, Z_W), lambda i, t: (i, t, Z_OFF // Z_W)),
                  pl.BlockSpec((None, D_MODEL, LANES), lambda i, t: (layer, 0, 0)),
                  pl.BlockSpec((None, CONV_A, W_QKV_A), lambda i, t: (layer, 0, 0)),
                  pl.BlockSpec((None, SUBLANES, LANES), lambda i, t: (layer, 0, 0)),
                  pl.BlockSpec((nb, H_A, DK_A, DV_A), lambda i, t: (i, 0, 0, 0)),
                  pl.BlockSpec((nb, HALO, W_QKV_A), lambda i, t: (i, 0, 0))],
        out_specs=[pl.BlockSpec((nb, tc, W_V_A), lambda i, t: (i, t, 0)),
                   pl.BlockSpec((nb, H_A, DK_A, DV_A), lambda i, t: (i, 0, 0, 0)),
                   pl.BlockSpec((nb, HALO, W_QKV_A), lambda i, t: (i, 0, 0))],
        out_shape=[jax.ShapeDtypeStruct((b, l, W_V_A), F32),
                   jax.ShapeDtypeStruct((b, H_A, DK_A, DV_A), F32),
                   jax.ShapeDtypeStruct((b, HALO, W_QKV_A), F32)],
        scratch_shapes=[pltpu.VMEM((nb, H_A, DK_A, DV_A), F32),
                        pltpu.VMEM((nb, tc + HALO, W_QKV_A), F32),
                        pltpu.VMEM((nb, cp, W_QK_A), F32),
                        pltpu.VMEM((nb, cp, W_QK_A), F32),
                        pltpu.VMEM((nb, cp, W_V_A), F32),
                        pltpu.VMEM((nb, cp, LANES), F32),
                        pltpu.VMEM((nb, cp, W_V_A), F32),
                        pltpu.VMEM((nb, cp, W_V_A), F32),
                        pltpu.VMEM((nb, 2 * cp, W_QK_A), BF16),
                        pltpu.VMEM((nb, cp, H_A * LANES), BF16),
                        pltpu.VMEM((nb, cp, W_QK_A), BF16),
                        pltpu.VMEM((nb, cp // CHUNK_A * SUBLANES, H_A * LANES), F32)],
        compiler_params=_cparams(("parallel", "arbitrary")),
        name="delta_mixer",
    )(x, h, h, wba, conv_a, prm, s0, db0)


def _sb_group(tiles, lm, bias, r0, chained, first_key_cols=1):
    zs, cs, tots = _sb_scores([(t[0], t[1], t[3]) for t in tiles], lm, bias, first_key_cols)
    return _sb_apply(zs, cs, tots, [t[2] for t in tiles], [t[3] for t in tiles], r0, chained)


def _sb_scores(tiles, lm, bias, first_key_cols):
    zs = [_dot_nt(q, kb) + bias for q, kb, _ in tiles]
    cs, tots = _sb_cumulate(zs, [t[2] for t in tiles], lm, first_key_cols)
    return zs, cs, tots


def _sb_cumulate(zs, valids, lm, first_key_cols):
    sps = [jnp.maximum(z, 0.0) + jnp.log(1.0 + jnp.exp(_neg_abs(z))) for z in zs]
    sps = [sp if v is None else jnp.where(v, sp, 0.0) for sp, v in zip(sps, valids)]
    later = [_dot(sp.astype(BF16), lm) for sp in sps]
    tots = [lt[:, 0:1] + jnp.sum(sp[:, 0:first_key_cols], axis=-1, keepdims=True) for lt, sp in zip(later, sps)]
    return [sp + lt for sp, lt in zip(sps, later)], tots


def _sb_apply(zs, cs, tots, vbs, valids, r0, chained):
    rs, ws = [], []
    r = r0
    for i, (z, c, tot, valid) in enumerate(zip(zs, cs, tots, valids)):
        r_prev = r if chained else r0[i]
        a = jnp.exp(z - c - r_prev)
        if valid is not None:
            a = jnp.where(valid, a, 0.0)
        ws.append(a.astype(BF16))
        r = r_prev + tot
        rs.append(r)
    return [_dot(a, vb) for a, vb in zip(ws, vbs)], rs


def _attn_prompt_kernel(bias_ref, q_ref, k_ref, v_ref, lm_ref, o_ref, kbf, vbf, acc, rsum, zbuf, cbuf, tbuf,
                        *, tq, tk, ts):
    h = pl.program_id(1)
    qi = pl.program_id(2)
    nsub = tq // ts
    ndiag = tq // tk

    @pl.when(qi == 0)
    def _():
        kbf[...] = k_ref[0].astype(BF16)
        vbf[...] = v_ref[0].astype(BF16)

    q = (q_ref[0] * (DH_B ** -0.5)).astype(BF16)
    qs = [q[s * ts:(s + 1) * ts] for s in range(nsub)]
    bias = bias_ref[h]
    lm = lm_ref[...]
    row = lax.broadcasted_iota(jnp.int32, (ts, tk), 0)
    col = lax.broadcasted_iota(jnp.int32, (ts, tk), 1)
    started = set()

    q0 = qi * tq

    def diag_subs(d):
        out = []
        for s in range(nsub):
            off = s * ts - d * tk
            if off + ts - 1 > 0:
                out.append((s, None if off >= tk else col < row + off))
        return out

    def diag_qk(d):
        kb = kbf[pl.ds(pl.multiple_of(q0 + d * tk, tk), tk), :]
        return [_dot_nt(qs[s], kb) + bias for s, _ in diag_subs(d)]

    def diag_finish(d, zs):
        cs, tots = _sb_cumulate(zs, [v for _, v in diag_subs(d)], lm, 1)
        return zs, cs, tots

    def diag_apply(d, scores):
        zs, cs, tots = scores
        subs = diag_subs(d)
        vb = vbf[pl.ds(pl.multiple_of(q0 + d * tk, tk), tk), :]
        r0 = [rsum[s * ts:(s + 1) * ts, :] if s in started else jnp.zeros((ts, 1), F32) for s, _ in subs]
        old = [acc[s * ts:(s + 1) * ts, :] if s in started else None for s, _ in subs]
        pvs, rs = _sb_apply(zs, cs, tots, [vb] * len(subs), [v for _, v in subs], r0, False)
        for (s, _), o, pv, r in zip(subs, old, pvs, rs):
            acc[s * ts:(s + 1) * ts, :] = pv if o is None else o + pv
            rsum[s * ts:(s + 1) * ts, :] = r
            started.add(s)

    scores = diag_finish(ndiag - 1, diag_qk(ndiag - 1))
    for d in reversed(range(ndiag)):
        z_next = diag_qk(d - 1) if d > 0 else None
        diag_apply(d, scores)
        if d > 0:
            scores = diag_finish(d - 1, z_next)

    n_past = qi * ndiag

    def past_block(i):
        return pl.multiple_of(jnp.maximum(q0 - (i + 1) * tk, 0), tk)

    rows = [slice(s * ts, (s + 1) * ts) for s in range(nsub)]

    def qk(i):
        kb = kbf[pl.ds(past_block(i), tk), :]
        return [_dot_nt(qs[s], kb) + bias for s in range(nsub)]

    def finish_scores(zs, slot):
        cs, tots = _sb_cumulate(zs, [None] * nsub, lm, 1)
        for r, z, c, t in zip(rows, zs, cs, tots):
            zbuf[slot, r, :] = z
            cbuf[slot, r, :] = c
            tbuf[slot, r, :] = t

    def apply(i, slot):
        vb = vbf[pl.ds(past_block(i), tk), :]
        pvs, rs = _sb_apply([zbuf[slot, r, :] for r in rows], [cbuf[slot, r, :] for r in rows],
                            [tbuf[slot, r, :] for r in rows], [vb] * nsub, [None] * nsub,
                            [rsum[r, :] for r in rows], False)
        for r, pv, rn in zip(rows, pvs, rs):
            acc[r, :] += pv
            rsum[r, :] = rn

    @pl.when(n_past > 0)
    def _():
        finish_scores(qk(0), 0)

    def body(j, carry):
        for slot in range(2):
            i = 2 * j + slot
            z_next = qk(i + 1)
            apply(i, slot)
            finish_scores(z_next, 1 - slot)
        return carry

    lax.fori_loop(0, n_past // 2, body, 0)
    o_ref[0] = acc[...]


def _later_ones(n):
    r = lax.broadcasted_iota(jnp.int32, (n, n), 0)
    c = lax.broadcasted_iota(jnp.int32, (n, n), 1)
    return (r > c).astype(BF16)


def attn_prompt(h, sb_bias):
    b, l, _ = h.shape
    tq = min(1024, l)
    tk = min(256, tq)
    ts = min(256, tq)
    assert l == tq or (tq // tk) % 2 == 0
    qb = QKVB_OFF // DH_B
    kern = functools.partial(_attn_prompt_kernel, tq=tq, tk=tk, ts=ts)
    grid_spec = pltpu.PrefetchScalarGridSpec(
        num_scalar_prefetch=0,
        grid=(b, H_B, l // tq),
        in_specs=[pl.BlockSpec(memory_space=pltpu.SMEM),
                  pl.BlockSpec((1, tq, DH_B), lambda i, hh, j: (i, j, qb + hh)),
                  pl.BlockSpec((1, l, DH_B), lambda i, hh, j: (i, 0, qb + H_B + hh)),
                  pl.BlockSpec((1, l, DH_B), lambda i, hh, j: (i, 0, qb + 2 * H_B + hh)),
                  pl.BlockSpec((tk, tk), lambda i, hh, j: (0, 0))],
        out_specs=pl.BlockSpec((1, tq, DH_B), lambda i, hh, j: (i, j, hh)),
        scratch_shapes=[pltpu.VMEM((l, DH_B), BF16),
                        pltpu.VMEM((l, DH_B), BF16),
                        pltpu.VMEM((tq, DH_B), F32),
                        pltpu.VMEM((tq, 1), F32),
                        pltpu.VMEM((2, tq, tk), F32),
                        pltpu.VMEM((2, tq, tk), F32),
                        pltpu.VMEM((2, tq, 1), F32)])
    return pl.pallas_call(
        kern,
        grid_spec=grid_spec,
        out_shape=jax.ShapeDtypeStruct((b, l, W_B), F32),
        compiler_params=_cparams(("parallel", "parallel", "arbitrary")),
        name="attn_prompt",
    )(sb_bias, h, h, h, _later_ones(tk))


def _attn_sample_kernel(pt_ref, bias_ref, qkv_ref, *rest, seq, pages_per_step, page_group, n_steps):
    g_pages = pages_per_step
    k_refs = rest[:g_pages]
    v_refs = rest[g_pages:2 * g_pages]
    lm_ref = rest[2 * g_pages]
    lx_ref = rest[2 * g_pages + 1]
    o_ref = rest[2 * g_pages + 2]
    qall, acc, rsum = rest[2 * g_pages + 3:]
    s = pl.program_id(1)
    rows = H_B * seq
    pcols = PAGE_SIZE * H_B
    rid = lax.broadcasted_iota(jnp.int32, (rows, 1), 0)
    bias = jnp.zeros((rows, 1), F32)
    for hh in range(H_B):
        bias = jnp.where((rid >= hh * seq) & (rid < (hh + 1) * seq), bias_ref[hh], bias)

    @pl.when(s == 0)
    def _():
        qkv = qkv_ref[0]
        q = qkv[:, 0:W_B] * (DH_B ** -0.5)
        qall[...] = jnp.concatenate([q[:, hh * DH_B:(hh + 1) * DH_B] for hh in range(H_B)], axis=0).astype(BF16)
        lane = lax.broadcasted_iota(jnp.int32, (seq, W_B), 1)
        parts = [jnp.where((lane >= hh * DH_B) & (lane < (hh + 1) * DH_B), q, 0.0) for hh in range(H_B)]
        qbd = jnp.concatenate(parts, axis=0).astype(BF16)
        zrows = jnp.zeros((PAGE_SIZE - seq, W_B), F32)
        k_own = jnp.concatenate([qkv[:, W_B:2 * W_B], zrows], axis=0).astype(BF16)
        v_own = jnp.concatenate([qkv[:, 2 * W_B:3 * W_B], zrows], axis=0).astype(BF16)
        key = lax.broadcasted_iota(jnp.int32, (rows, PAGE_SIZE), 1)
        qpos = lax.broadcasted_iota(jnp.int32, (rows, PAGE_SIZE), 0) % seq
        pvs, rs = _sb_group([(qbd, k_own, v_own, key < qpos)], lm_ref[...], bias, jnp.zeros((rows, 1), F32), True)
        acc[...] = jnp.concatenate(
            [pvs[0][hh * seq:(hh + 1) * seq, hh * DH_B:(hh + 1) * DH_B] for hh in range(H_B)], axis=0)
        rsum[...] = rs[0]

    own = (lax.broadcasted_iota(jnp.int32, (rows, pcols), 1) % H_B
           == lax.broadcasted_iota(jnp.int32, (rows, pcols), 0) // seq)
    qa = qall[...]
    lx = lx_ref[...]
    r = rsum[...]
    total = acc[...]
    for g0 in reversed(range(0, g_pages, page_group)):
        tiles = [(qa, k_refs[gi][0, 0].astype(BF16), v_refs[gi][0, 0].astype(BF16), own)
                 for gi in reversed(range(g0, g0 + page_group))]
        pvs, rs = _sb_group(tiles, lx, bias, r, True, H_B)
        r = rs[-1]
        for pv in pvs:
            total = total + pv
    acc[...] = total
    rsum[...] = r

    @pl.when(s == n_steps - 1)
    def _():
        a = acc[...]
        o_ref[0] = jnp.concatenate([a[hh * seq:(hh + 1) * seq, :] for hh in range(H_B)], axis=1)


def attn_sample(h, cache_k, cache_v, page_table, sb_bias, layer):
    b, seq, _ = h.shape
    depth, n_pool = cache_k.shape[:2]
    n_pages = page_table.shape[0] // b
    g_pages = min(32, n_pages)
    n_steps = n_pages // g_pages
    rows = H_B * seq
    pcols = PAGE_SIZE * H_B
    ck = cache_k.reshape(depth, n_pool, pcols, DH_B)
    cv = cache_v.reshape(depth, n_pool, pcols, DH_B)
    kidx = lax.broadcasted_iota(jnp.int32, (pcols, pcols), 0) // H_B
    lexp = (kidx > kidx.T).astype(BF16)

    def page_map(gi):
        def index_map(i, s, pt):
            return (layer, pt[i * n_pages + (n_steps - 1 - s) * g_pages + gi], 0, 0)
        return index_map

    page_specs = [pl.BlockSpec((1, 1, pcols, DH_B), page_map(gi)) for gi in range(g_pages)]
    kern = functools.partial(_attn_sample_kernel, seq=seq, pages_per_step=g_pages, page_group=min(8, g_pages),
                             n_steps=n_steps)
    grid_spec = pltpu.PrefetchScalarGridSpec(
        num_scalar_prefetch=1,
        grid=(b, n_steps),
        in_specs=([pl.BlockSpec(memory_space=pltpu.SMEM),
                   pl.BlockSpec((1, seq, QKVB_W), lambda i, s, pt: (i, 0, QKVB_OFF // QKVB_W))]
                  + page_specs + page_specs
                  + [pl.BlockSpec((PAGE_SIZE, PAGE_SIZE), lambda i, s, pt: (0, 0)),
                     pl.BlockSpec((pcols, pcols), lambda i, s, pt: (0, 0))]),
        out_specs=pl.BlockSpec((1, seq, W_B), lambda i, s, pt: (i, 0, 0)),
        scratch_shapes=[pltpu.VMEM((rows, DH_B), BF16),
                        pltpu.VMEM((rows, DH_B), F32),
                        pltpu.VMEM((rows, 1), F32)])
    return pl.pallas_call(
        kern,
        grid_spec=grid_spec,
        out_shape=jax.ShapeDtypeStruct((b, seq, W_B), F32),
        compiler_params=_cparams(("parallel", "arbitrary")),
        name="attn_sample",
    )(page_table, sb_bias, h, *([ck] * g_pages), *([cv] * g_pages), _later_ones(PAGE_SIZE), lexp)


def _merge_kernel(x_ref, g_ref, c_ref, oa_ref, ob_ref, wpa_ref, wpb_ref, wpc_ref, wo_ref, conv_ref, ln_ref, sb0_ref,
                  o_ref, tail_ref, ext_scr, *, tm):
    t = pl.program_id(1)

    @pl.when(t == 0)
    def _():
        ext_scr[0:HALO, :] = sb0_ref[0]

    cc = c_ref[0]
    u = cc[:, 2 * W_C:] * cc[:, :W_C]
    ext_scr[HALO:HALO + tm, :] = u
    w = conv_ref[...]
    y = ext_scr[HALO - 2:HALO - 2 + tm, :] * w[0:1, :] + ext_scr[HALO - 1:HALO - 1 + tm, :] * w[1:2, :] + u * w[2:3, :]
    tail = ext_scr[tm:tm + HALO, :]
    ext_scr[0:HALO, :] = tail
    tail_ref[0] = tail
    o_c = cc[:, W_C:2 * W_C] * y

    g = g_ref[0]
    merged = (_sigmoid(g[:, :D_MODEL]) * _dot(oa_ref[0].astype(BF16), wpa_ref[...])
              + _sigmoid(g[:, D_MODEL:2 * D_MODEL]) * _dot(ob_ref[0].astype(BF16), wpb_ref[...])
              + _sigmoid(g[:, 2 * D_MODEL:]) * _dot(o_c.astype(BF16), wpc_ref[...]))
    r = DEEPNORM_ALPHA * x_ref[0] + _dot(merged.astype(BF16), wo_ref[...])
    o_ref[0] = _layer_norm(r, ln_ref[0:1, :], ln_ref[1:2, :])


def merge(x, h, o_a, o_b, wpa, wpb, wpc, wo, conv_c, ln, sb0, layer):
    b, l, _ = x.shape
    tm = min(512, l)
    kern = functools.partial(_merge_kernel, tm=tm)
    const = lambda i, t: (layer, 0, 0)
    return pl.pallas_call(
        kern,
        grid=(b, l // tm),
        in_specs=[pl.BlockSpec((1, tm, D_MODEL), lambda i, t: (i, t, 0)),
                  pl.BlockSpec((1, tm, G_W), lambda i, t: (i, t, G_OFF // G_W)),
                  pl.BlockSpec((1, tm, C_W), lambda i, t: (i, t, C_OFF // C_W)),
                  pl.BlockSpec((1, tm, W_V_A), lambda i, t: (i, t, 0)),
                  pl.BlockSpec((1, tm, W_B), lambda i, t: (i, t, 0)),
                  pl.BlockSpec((None, W_V_A, D_MODEL), const),
                  pl.BlockSpec((None, W_B, D_MODEL), const),
                  pl.BlockSpec((None, W_C, D_MODEL), const),
                  pl.BlockSpec((None, D_MODEL, D_MODEL), const),
                  pl.BlockSpec((None, CONV_C, W_C), const),
                  pl.BlockSpec((None, 2, D_MODEL), const),
                  pl.BlockSpec((1, HALO, W_C), lambda i, t: (i, 0, 0))],
        out_specs=[pl.BlockSpec((1, tm, D_MODEL), lambda i, t: (i, t, 0)),
                   pl.BlockSpec((1, HALO, W_C), lambda i, t: (i, 0, 0))],
        out_shape=[jax.ShapeDtypeStruct((b, l, D_MODEL), F32),
                   jax.ShapeDtypeStruct((b, HALO, W_C), F32)],
        scratch_shapes=[pltpu.VMEM((tm + HALO, W_C), F32)],
        compiler_params=_cparams(("parallel", "arbitrary")),
        name="merge",
    )(x, h, h, o_a, o_b, wpa, wpb, wpc, wo, conv_c, ln, sb0)


def _moe_kernel(x_ref, wr_ref, br_ref, wgu_ref, wdn_ref, ln_ref, o_ref, xb, gates, acc, *, tm, eps):
    e = pl.program_id(1)
    lane = lax.broadcasted_iota(jnp.int32, (tm, LANES), 1)

    @pl.when(e == 0)
    def _():
        x = x_ref[...]
        xh, xl = _split2(x)
        wh, wl = _split2(wr_ref[...])
        logits = _dot(xh, wh) + _dot(xh, wl) + _dot(xl, wh) + br_ref[...]
        lanef = lane.astype(F32)
        big = float(LANES)
        is_g = lane < N_GROUPS
        gl = jnp.where(is_g, logits, -jnp.inf)
        gmax = jnp.max(gl, axis=-1, keepdims=True)
        gsel = jnp.min(jnp.where(gl == gmax, lanef, big), axis=-1, keepdims=True)
        pg_sel = 1.0 / jnp.sum(jnp.where(is_g, jnp.exp(gl - gmax), 0.0), axis=-1, keepdims=True)
        lo = N_GROUPS + gsel * EXPERTS_PER_GROUP
        ev = jnp.where((lanef >= lo) & (lanef < lo + EXPERTS_PER_GROUP), logits, -jnp.inf)
        v1 = jnp.max(ev, axis=-1, keepdims=True)
        i1 = jnp.min(jnp.where(ev == v1, lanef, big), axis=-1, keepdims=True)
        ev2 = jnp.where(lanef == i1, -jnp.inf, ev)
        v2 = jnp.max(ev2, axis=-1, keepdims=True)
        i2 = jnp.min(jnp.where(ev2 == v2, lanef, big), axis=-1, keepdims=True)
        e2 = jnp.exp(v2 - v1)
        den = 1.0 + e2
        gates[...] = jnp.where(lanef == i1, pg_sel / den, jnp.where(lanef == i2, pg_sel * e2 / den, 0.0))
        xb[...] = xh
        acc[...] = jnp.zeros((tm, D_MODEL), F32)

    xv = xb[...]
    g_all = gates[...]
    hhs = [_dot(xv, wgu_ref[j]) for j in range(eps)]
    acts = []
    for j, hh in enumerate(hhs):
        ge = jnp.sum(jnp.where(lane == N_GROUPS + e * eps + j, g_all, 0.0), axis=-1, keepdims=True)
        acts.append((_silu(hh[:, :D_EXPERT]) * hh[:, D_EXPERT:] * ge).astype(BF16))
    acc[...] += _dot(jnp.concatenate(acts, axis=1), wdn_ref[...].reshape(eps * D_EXPERT, D_MODEL))

    @pl.when(e == N_EXPERTS // eps - 1)
    def _():
        r = DEEPNORM_ALPHA * x_ref[...] + acc[...]
        o_ref[...] = _layer_norm(r, ln_ref[0:1, :], ln_ref[1:2, :])


def moe(x2d, wr, br, wgu, wdn, ln, layer):
    t_rows = x2d.shape[0]
    tm = min(1024, t_rows)
    eps = 4
    kern = functools.partial(_moe_kernel, tm=tm, eps=eps)
    return pl.pallas_call(
        kern,
        grid=(t_rows // tm, N_EXPERTS // eps),
        in_specs=[pl.BlockSpec((tm, D_MODEL), lambda i, e: (i, 0)),
                  pl.BlockSpec((None, D_MODEL, LANES), lambda i, e: (layer, 0, 0)),
                  pl.BlockSpec((None, 1, LANES), lambda i, e: (layer, 0, 0)),
                  pl.BlockSpec((eps, D_MODEL, 2 * D_EXPERT), lambda i, e: (layer * (N_EXPERTS // eps) + e, 0, 0)),
                  pl.BlockSpec((eps, D_EXPERT, D_MODEL), lambda i, e: (layer * (N_EXPERTS // eps) + e, 0, 0)),
                  pl.BlockSpec((None, 2, D_MODEL), lambda i, e: (layer, 0, 0))],
        out_specs=pl.BlockSpec((tm, D_MODEL), lambda i, e: (i, 0)),
        out_shape=jax.ShapeDtypeStruct((t_rows, D_MODEL), F32),
        scratch_shapes=[pltpu.VMEM((tm, D_MODEL), BF16),
                        pltpu.VMEM((tm, LANES), F32),
                        pltpu.VMEM((tm, D_MODEL), F32)],
        compiler_params=_cparams(("parallel", "arbitrary")),
        name="moe",
    )(x2d, wr, br, wgu, wdn, ln)


def _prep_params(w_in, conv_a, a_log, dt_bias, norm_a, conv_c, w_pa, w_pb, w_pc, w_o, ln_g, ln_b,
                 w_rg, b_rg, w_re, b_re, w_gu, w_down):
    depth = w_in.shape[0]
    o_z = W_QKV_A
    o_ba = o_z + W_V_A
    o_qb = o_ba + 2 * H_A
    o_c = o_qb + 3 * W_B
    o_g = o_c + 3 * W_C
    w_main = jnp.concatenate([w_in[:, :, o_g:], w_in[:, :, :W_QKV_A], w_in[:, :, o_c:o_g], w_in[:, :, o_qb:o_c],
                              w_in[:, :, o_z:o_ba]], axis=2).astype(BF16)
    w_ba = jnp.pad(w_in[:, :, o_ba:o_qb], ((0, 0), (0, 0), (0, LANES - 2 * H_A))).astype(BF16)
    prm = jnp.zeros((depth, SUBLANES, LANES), F32)
    prm = prm.at[:, 0, H_A:2 * H_A].set(a_log).at[:, 1, H_A:2 * H_A].set(dt_bias).at[:, 2, :DV_A].set(norm_a)
    pad_r = LANES - N_GROUPS - N_EXPERTS
    w_r = jnp.pad(jnp.concatenate([w_rg, w_re], axis=2), ((0, 0), (0, 0), (0, pad_r)))
    b_r = jnp.pad(jnp.concatenate([b_rg, b_re], axis=1), ((0, 0), (0, pad_r))).reshape(depth, 1, LANES)
    return dict(
        w_main=w_main, w_ba=w_ba, conv_a=conv_a, prm=prm, conv_c=conv_c,
        w_pa=w_pa.astype(BF16), w_pb=w_pb.astype(BF16), w_pc=w_pc.astype(BF16), w_o=w_o.astype(BF16),
        ln0=jnp.stack([ln_g[:, 0], ln_b[:, 0]], axis=1), ln1=jnp.stack([ln_g[:, 1], ln_b[:, 1]], axis=1),
        w_r=w_r, b_r=b_r,
        w_gu=w_gu.astype(BF16).reshape(depth * N_EXPERTS, D_MODEL, 2 * D_EXPERT),
        w_down=w_down.astype(BF16).reshape(depth * N_EXPERTS, D_EXPERT, D_MODEL))


def _pad_tail(buf):
    return jnp.pad(buf, ((0, 0), (HALO - buf.shape[1], 0), (0, 0)))


def _layer(x, p, s0, db0, sb0, attn_fn, layer, depth, k_buf, v_buf):
    b, l, _ = x.shape
    h, k_buf, v_buf = in_proj(x.reshape(b * l, D_MODEL), p["w_main"], layer, depth, k_buf, v_buf)
    h = h.reshape(b, l, H_COLS)
    o_a, s_new, dtail = delta_mixer(x, h, p["w_ba"], p["conv_a"], p["prm"], s0, _pad_tail(db0), layer)
    o_b = attn_fn(h)
    x1, stail = merge(x, h, o_a, o_b, p["w_pa"], p["w_pb"], p["w_pc"], p["w_o"], p["conv_c"], p["ln0"],
                      _pad_tail(sb0), layer)
    x2 = moe(x1.reshape(b * l, D_MODEL), p["w_r"], p["b_r"], p["w_gu"], p["w_down"], p["ln1"], layer)
    x2 = x2.reshape(b, l, D_MODEL)
    return (x2, s_new, dtail[:, HALO - (CONV_A - 1):], stail[:, HALO - (CONV_C - 1):]), k_buf, v_buf


def kernel(x_prompt, x_sample, cache_k, cache_v, state_delta, state_dconv, state_sconv, page_table, w_in, conv_a, a_log, dt_bias, norm_a, sb_bias, conv_c, w_pa, w_pb, w_pc, w_o, ln_g, ln_b, w_rg, b_rg, w_re, b_re, w_gu, w_down):
    bp, lp, _ = x_prompt.shape
    bs, ls, _ = x_sample.shape
    depth = w_in.shape[0]
    pt = page_table.reshape(-1).astype(jnp.int32)
    xp, xs = x_prompt, x_sample
    outs_p = [[] for _ in range(3)]
    outs_s = [[] for _ in range(3)]
    pk = pv = sk = sv = None
    p = _prep_params(w_in, conv_a, a_log, dt_bias, norm_a, conv_c, w_pa, w_pb, w_pc, w_o, ln_g, ln_b,
                     w_rg, b_rg, w_re, b_re, w_gu, w_down)
    for l in range(depth):
        bias = sb_bias[l]
        res, pk, pv = _layer(xp, p,
                             jnp.zeros((bp, H_A, DK_A, DV_A), F32),
                             jnp.zeros((bp, CONV_A - 1, W_QKV_A), F32),
                             jnp.zeros((bp, CONV_C - 1, W_C), F32),
                             lambda h: attn_prompt(h, bias), l, depth, pk, pv)
        xp = res[0]
        for acc_list, r in zip(outs_p, res[1:]):
            acc_list.append(r)
        res, sk, sv = _layer(xs, p, state_delta[l], state_dconv[l], state_sconv[l],
                             lambda h: attn_sample(h, cache_k, cache_v, pt, bias, l), l, depth, sk, sv)
        xs = res[0]
        for acc_list, r in zip(outs_s, res[1:]):
            acc_list.append(r)
    return (xp, xs, *[jnp.stack(o) for o in outs_p],
            pk.reshape(depth, bp, lp, H_B, DH_B), pv.reshape(depth, bp, lp, H_B, DH_B),
            *[jnp.stack(o) for o in outs_s],
            sk.reshape(depth, bs, ls, H_B, DH_B), sv.reshape(depth, bs, ls, H_B, DH_B))
```

```python
import functools

import jax
import jax.numpy as jnp
from jax import lax
from jax.experimental import pallas as pl
from jax.experimental.pallas import tpu as pltpu

F32 = jnp.float32
BF16 = jnp.bfloat16

D_MODEL = 1024
DEPTH = 4
H_A = 4
DK_A = 128
DV_A = 128
CONV_A = 4
CHUNK_A = 64
H_B = 4
DH_B = 128
W_C = 512
CONV_C = 3
N_GROUPS = 4
EXPERTS_PER_GROUP = 4
N_EXPERTS = N_GROUPS * EXPERTS_PER_GROUP
D_EXPERT = 256
PAGE_SIZE = 128

W_QK_A = H_A * DK_A
W_V_A = H_A * DV_A
W_QKV_A = 2 * W_QK_A + W_V_A
W_B = H_B * DH_B

DEEPNORM_ALPHA = (2.0 * DEPTH) ** 0.25
LN_EPS = 1e-5
RMS_EPS = 1e-6

G_OFF, G_W = 0, 3 * D_MODEL
QKVA_OFF, QKVA_W = 3072, W_QKV_A
C_OFF, C_W = 4608, 3 * W_C
QKVB_OFF, QKVB_W = 6144, 3 * W_B
Z_OFF, Z_W = 7680, W_V_A
H_COLS = 8192
LANES = 128
SUBLANES = 8
HALO = SUBLANES
PREP_GROUP = 32

VMEM_LIMIT = 56 * 1024 * 1024


def _cparams(sem):
    return pltpu.CompilerParams(dimension_semantics=sem, vmem_limit_bytes=VMEM_LIMIT)


def _dot(a, b):
    return jnp.dot(a, b, preferred_element_type=F32)


def _dot_nt(a, b):
    return lax.dot_general(a, b, (((1,), (1,)), ((), ())), preferred_element_type=F32)


def _dot_tn(a, b):
    return lax.dot_general(a, b, (((0,), (0,)), ((), ())), preferred_element_type=F32)


def _split2(x):
    hi = x.astype(BF16)
    lo = (x - hi.astype(F32)).astype(BF16)
    return hi, lo


def _split3(x):
    hi = x.astype(BF16)
    r = x - hi.astype(F32)
    mid = r.astype(BF16)
    lo = (r - mid.astype(F32)).astype(BF16)
    return hi, mid, lo


def _mm_sel(sel, x):
    hi, mid, lo = _split3(x)
    return _dot(sel, hi) + _dot(sel, mid) + _dot(sel, lo)


def _neg_abs(x):
    bits = lax.bitcast_convert_type(x, jnp.uint32) | jnp.uint32(0x80000000)
    return lax.bitcast_convert_type(bits, F32)


def _softplus(x):
    return jnp.maximum(x, 0.0) + jnp.log1p(jnp.exp(-jnp.abs(x)))


def _sigmoid(x):
    return 1.0 / (1.0 + jnp.exp(-x))


def _silu(x):
    return x * _sigmoid(x)


def _layer_norm(r, g, b):
    mu = jnp.mean(r, axis=-1, keepdims=True)
    d = r - mu
    var = jnp.mean(d * d, axis=-1, keepdims=True)
    return d * lax.rsqrt(var + LN_EPS) * g + b


K_COL = QKVB_OFF + W_B
V_COL = QKVB_OFF + 2 * W_B


def _inproj_kernel(x_ref, w_ref, *refs, tm, tn):
    o_ref, k_ref, v_ref = refs[-3:]
    j = pl.program_id(1)
    o_ref[...] = _dot(x_ref[...].astype(BF16), w_ref[...])

    def rows_out(ref, col):
        @pl.when(j == col // tn)
        def _():
            for hh in range(H_B):
                c0 = col % tn + hh * DH_B
                ref[0, pl.ds(hh, tm, stride=H_B), :] = o_ref[:, c0:c0 + DH_B]

    rows_out(k_ref, K_COL)
    rows_out(v_ref, V_COL)


def in_proj(x2d, w, layer, depth, k_buf, v_buf):
    t_rows = x2d.shape[0]
    tm = min(1024, t_rows)
    tn = 2048
    assert K_COL % tn + W_B <= tn and V_COL % tn + W_B <= tn
    rows_shape = jax.ShapeDtypeStruct((depth, t_rows * H_B, DH_B), F32)
    rows_spec = pl.BlockSpec((1, tm * H_B, DH_B), lambda i, j: (layer, i, 0))
    carried = [] if k_buf is None else [k_buf, v_buf]
    return pl.pallas_call(
        functools.partial(_inproj_kernel, tm=tm, tn=tn),
        grid=(t_rows // tm, H_COLS // tn),
        in_specs=[pl.BlockSpec((tm, D_MODEL), lambda i, j: (i, 0)),
                  pl.BlockSpec((None, D_MODEL, tn), lambda i, j: (layer, 0, j))]
                 + [pl.BlockSpec(memory_space=pl.ANY)] * len(carried),
        out_specs=[pl.BlockSpec((tm, tn), lambda i, j: (i, j)), rows_spec, rows_spec],
        out_shape=[jax.ShapeDtypeStruct((t_rows, H_COLS), F32), rows_shape, rows_shape],
        input_output_aliases={2: 1, 3: 2} if carried else {},
        compiler_params=_cparams(("parallel", "arbitrary")),
        name="in_proj",
    )(x2d, w, *carried)


def _delta_kernel(x_ref, qkv_ref, z_ref, wba_ref, conv_ref, prm_ref, s0_ref, db0_ref,
                  o_ref, s_out_ref, tail_ref,
                  s_scr, ext_scr, q_scr, k_scr, v_scr, bg_scr, o_scr, u_scr, wq_scr, qk_scr, kd_scr, gl_scr,
                  *, nb, tc, cp):
    c = CHUNK_A
    t = pl.program_id(1)

    @pl.when(t == 0)
    def _():
        s_scr[...] = s0_ref[...]
        ext_scr[:, 0:HALO, :] = db0_ref[...]

    w = conv_ref[...]
    for bb in range(nb):
        u = qkv_ref[bb]
        ext_scr[bb, HALO:HALO + tc, :] = u
        y = (ext_scr[bb, HALO - 3:HALO - 3 + tc, :] * w[0:1, :] + ext_scr[bb, HALO - 2:HALO - 2 + tc, :] * w[1:2, :]
             + ext_scr[bb, HALO - 1:HALO - 1 + tc, :] * w[2:3, :] + u * w[3:4, :])
        tail = ext_scr[bb, tc:tc + HALO, :]
        ext_scr[bb, 0:HALO, :] = tail
        tail_ref[bb] = tail
        y = _silu(y)

        ba = _dot(x_ref[bb].astype(BF16), wba_ref[...])
        beta = _sigmoid(ba)
        g = -jnp.exp(prm_ref[0:1, :]) * _softplus(ba + prm_ref[1:2, :])
        lane = lax.broadcasted_iota(jnp.int32, (tc, LANES), 1)
        bg = jnp.where(lane < H_A, beta, g)

        if cp > tc:
            q_scr[bb, tc:cp, :] = jnp.zeros((cp - tc, W_QK_A), F32)
            k_scr[bb, tc:cp, :] = jnp.zeros((cp - tc, W_QK_A), F32)
            v_scr[bb, tc:cp, :] = jnp.zeros((cp - tc, W_V_A), F32)
            bg_scr[bb, tc:cp, :] = jnp.zeros((cp - tc, LANES), F32)
        bg_scr[bb, 0:tc, :] = bg
        for h in range(H_A):
            qh = y[:, h * DK_A:(h + 1) * DK_A]
            kh = y[:, W_QK_A + h * DK_A:W_QK_A + (h + 1) * DK_A]
            qn = qh * lax.rsqrt(jnp.sum(qh * qh, axis=-1, keepdims=True) + RMS_EPS) * (DK_A ** -0.5)
            kn = kh * lax.rsqrt(jnp.sum(kh * kh, axis=-1, keepdims=True) + RMS_EPS)
            q_scr[bb, 0:tc, h * DK_A:(h + 1) * DK_A] = qn
            k_scr[bb, 0:tc, h * DK_A:(h + 1) * DK_A] = kn
        v_scr[bb, 0:tc, :] = y[:, 2 * W_QK_A:]

    ri = lax.broadcasted_iota(jnp.int32, (c, c), 0)
    ci = lax.broadcasted_iota(jnp.int32, (c, c), 1)
    tril = (ri >= ci).astype(BF16)
    triu_f = (ri <= ci).astype(F32)
    ones = jnp.ones((c, c), BF16)

    def prep_many(ins):
        qcs, kcs, vcs, betas, gs = zip(*ins)
        gbs = [jnp.broadcast_to(g_c, (c, LANES)) for g_c in gs]
        gcs = [_mm_sel(tril, gb) for gb in gbs]
        grows = [_mm_sel(ones, gb[:, :c] * triu_f) for gb in gbs]
        decs = [jnp.exp(jnp.where(ri >= ci, gc[:, :c] - gr, -jnp.inf)) for gc, gr in zip(gcs, grows)]
        kbs = [kc * b for kc, b in zip(kcs, betas)]
        kqs = [_dot_nt(jnp.concatenate([kb, qc], axis=0).astype(BF16), kc.astype(BF16))
               for kb, qc, kc in zip(kbs, qcs, kcs)]
        qks = [kq[c:] * dec for kq, dec in zip(kqs, decs)]
        egcs = [jnp.exp(gc) for gc in gcs]
        rhss = [jnp.concatenate([vc * b, kb * egc], axis=1) for vc, b, kb, egc in zip(vcs, betas, kbs, egcs)]
        ns = [-jnp.where(ri > ci, kq[:c] * dec, 0.0) for kq, dec in zip(kqs, decs)]
        yys = ns
        for _ in range(5):
            nbs = [n.astype(BF16) for n in ns]
            ns = [_dot(nb, nb) for nb in nbs]
            prods = [_dot(yy.astype(BF16), n.astype(BF16)) for yy, n in zip(yys, ns)]
            yys = [yy + n + p for yy, n, p in zip(yys, ns, prods)]
        sols = [rhs + _dot(yy.astype(BF16), rhs.astype(BF16)) for rhs, yy in zip(rhss, yys)]
        gls = [gc[c - 1:c, :] for gc in gcs]
        outs = []
        for sol, qc, kc, qk, egc, gc, gl in zip(sols, qcs, kcs, qks, egcs, gcs, gls):
            wq = jnp.concatenate([sol[:, DV_A:], qc * egc], axis=0).astype(BF16)
            k_dec = (kc * jnp.exp(gl - gc)).astype(BF16)
            outs.append((sol[:, :DV_A], wq, qk.astype(BF16), k_dec,
                         jnp.broadcast_to(jnp.exp(gl), (SUBLANES, LANES))))
        return outs

    n_chunks = cp // c
    problems = [(bb, ic, h) for bb in range(nb) for ic in range(n_chunks) for h in range(H_A)]
    for p0 in range(0, len(problems), PREP_GROUP):
        group = problems[p0:p0 + PREP_GROUP]
        ins = []
        for bb, ic, h in group:
            rows = slice(ic * c, (ic + 1) * c)
            ins.append((q_scr[bb, rows, h * DK_A:(h + 1) * DK_A], k_scr[bb, rows, h * DK_A:(h + 1) * DK_A],
                        v_scr[bb, rows, h * DV_A:(h + 1) * DV_A], bg_scr[bb, rows, h:h + 1],
                        bg_scr[bb, rows, H_A + h:H_A + h + 1]))
        for (bb, ic, h), (uu, wq, qk, k_dec, egl) in zip(group, prep_many(ins)):
            rows = slice(ic * c, (ic + 1) * c)
            u_scr[bb, rows, h * DV_A:(h + 1) * DV_A] = uu
            wq_scr[bb, 2 * ic * c:2 * (ic + 1) * c, h * DK_A:(h + 1) * DK_A] = wq
            qk_scr[bb, rows, h * LANES:h * LANES + c] = qk
            kd_scr[bb, rows, h * DK_A:(h + 1) * DK_A] = k_dec
            gl_scr[bb, ic * SUBLANES:(ic + 1) * SUBLANES, h * LANES:(h + 1) * LANES] = egl

    chains = [(bb, h) for bb in range(nb) for h in range(H_A)]
    ss = [s_scr[bb, h] for bb, h in chains]
    for ic in range(n_chunks):
        rows = slice(ic * c, (ic + 1) * c)
        us = [u_scr[bb, rows, h * DV_A:(h + 1) * DV_A] for bb, h in chains]
        wqs = [wq_scr[bb, 2 * ic * c:2 * (ic + 1) * c, h * DK_A:(h + 1) * DK_A] for bb, h in chains]
        qks = [qk_scr[bb, rows, h * LANES:h * LANES + c] for bb, h in chains]
        kds = [kd_scr[bb, rows, h * DK_A:(h + 1) * DK_A] for bb, h in chains]
        egls = [gl_scr[bb, ic * SUBLANES:ic * SUBLANES + 1, h * LANES:(h + 1) * LANES] for bb, h in chains]
        wss = [_dot(wq, s.astype(BF16)) for wq, s in zip(wqs, ss)]
        vns = [(uu - ws[:c]).astype(BF16) for uu, ws in zip(us, wss)]
        ocs = [ws[c:] + _dot(qk, vn) for ws, qk, vn in zip(wss, qks, vns)]
        ss = [s * egl + _dot_tn(kd, vn) for s, egl, kd, vn in zip(ss, egls, kds, vns)]
        for (bb, h), o_c in zip(chains, ocs):
            o_scr[bb, rows, h * DV_A:(h + 1) * DV_A] = o_c
    for (bb, h), s in zip(chains, ss):
        s_scr[bb, h] = s
        s_out_ref[bb, h] = s

    nw = prm_ref[2:3, :]
    for bb in range(nb):
        z = z_ref[bb]
        for h in range(H_A):
            oh = o_scr[bb, 0:tc, h * DV_A:(h + 1) * DV_A]
            zh = z[:, h * DV_A:(h + 1) * DV_A]
            oh = oh * lax.rsqrt(jnp.mean(oh * oh, axis=-1, keepdims=True) + RMS_EPS)
            o_ref[bb, :, h * DV_A:(h + 1) * DV_A] = oh * nw * _silu(zh)


def delta_mixer(x, h, wba, conv_a, prm, s0, db0, layer):
    b, l, _ = x.shape
    tc = min(256, l)
    cp = max(tc, CHUNK_A)
    nt = l // tc
    nb = min(b, max(1, PREP_GROUP // (H_A * (cp // CHUNK_A))) if l < CHUNK_A else 2)
    assert b % nb == 0
    kern = functools.partial(_delta_kernel, nb=nb, tc=tc, cp=cp)
    return pl.pallas_call(
        kern,
        grid=(b // nb, nt),
        in_specs=[pl.BlockSpec((nb, tc, D_MODEL), lambda i, t: (i, t, 0)),
                  pl.BlockSpec((nb, tc, QKVA_W), lambda i, t: (i, t, QKVA_OFF // QKVA_W)),
                  pl.BlockSpec((nb, tc, Z_W), lambda i, t: (i, t, Z_OFF // Z_W)),
                  pl.BlockSpec((None, D_MODEL, LANES), lambda i, t: (layer, 0, 0)),
                  pl.BlockSpec((None, CONV_A, W_QKV_A), lambda i, t: (layer, 0, 0)),
                  pl.BlockSpec((None, SUBLANES, LANES), lambda i, t: (layer, 0, 0)),
                  pl.BlockSpec((nb, H_A, DK_A, DV_A), lambda i, t: (i, 0, 0, 0)),
                  pl.BlockSpec((nb, HALO, W_QKV_A), lambda i, t: (i, 0, 0))],
        out_specs=[pl.BlockSpec((nb, tc, W_V_A), lambda i, t: (i, t, 0)),
                   pl.BlockSpec((nb, H_A, DK_A, DV_A), lambda i, t: (i, 0, 0, 0)),
                   pl.BlockSpec((nb, HALO, W_QKV_A), lambda i, t: (i, 0, 0))],
        out_shape=[jax.ShapeDtypeStruct((b, l, W_V_A), F32),
                   jax.ShapeDtypeStruct((b, H_A, DK_A, DV_A), F32),
                   jax.ShapeDtypeStruct((b, HALO, W_QKV_A), F32)],
        scratch_shapes=[pltpu.VMEM((nb, H_A, DK_A, DV_A), F32),
                        pltpu.VMEM((nb, tc + HALO, W_QKV_A), F32),
                        pltpu.VMEM((nb, cp, W_QK_A), F32),
                        pltpu.VMEM((nb, cp, W_QK_A), F32),
                        pltpu.VMEM((nb, cp, W_V_A), F32),
                        pltpu.VMEM((nb, cp, LANES), F32),
                        pltpu.VMEM((nb, cp, W_V_A), F32),
                        pltpu.VMEM((nb, cp, W_V_A), F32),
                        pltpu.VMEM((nb, 2 * cp, W_QK_A), BF16),
                        pltpu.VMEM((nb, cp, H_A * LANES), BF16),
                        pltpu.VMEM((nb, cp, W_QK_A), BF16),
                        pltpu.VMEM((nb, cp // CHUNK_A * SUBLANES, H_A * LANES), F32)],
        compiler_params=_cparams(("parallel", "arbitrary")),
        name="delta_mixer",
    )(x, h, h, wba, conv_a, prm, s0, db0)


def _sb_group(tiles, lm, bias, r0, chained, first_key_cols=1):
    zs, cs, tots = _sb_scores([(t[0], t[1], t[3]) for t in tiles], lm, bias, first_key_cols)
    return _sb_apply(zs, cs, tots, [t[2] for t in tiles], [t[3] for t in tiles], r0, chained)


def _sb_scores(tiles, lm, bias, first_key_cols):
    zs = [_dot_nt(q, kb) + bias for q, kb, _ in tiles]
    cs, tots = _sb_cumulate(zs, [t[2] for t in tiles], lm, first_key_cols)
    return zs, cs, tots


def _sb_cumulate(zs, valids, lm, first_key_cols):
    sps = [jnp.maximum(z, 0.0) + jnp.log(1.0 + jnp.exp(_neg_abs(z))) for z in zs]
    sps = [sp if v is None else jnp.where(v, sp, 0.0) for sp, v in zip(sps, valids)]
    later = [_dot(sp.astype(BF16), lm) for sp in sps]
    tots = [lt[:, 0:1] + jnp.sum(sp[:, 0:first_key_cols], axis=-1, keepdims=True) for lt, sp in zip(later, sps)]
    return [sp + lt for sp, lt in zip(sps, later)], tots


def _sb_apply(zs, cs, tots, vbs, valids, r0, chained):
    rs, ws = [], []
    r = r0
    for i, (z, c, tot, valid) in enumerate(zip(zs, cs, tots, valids)):
        r_prev = r if chained else r0[i]
        a = jnp.exp(z - c - r_prev)
        if valid is not None:
            a = jnp.where(valid, a, 0.0)
        ws.append(a.astype(BF16))
        r = r_prev + tot
        rs.append(r)
    return [_dot(a, vb) for a, vb in zip(ws, vbs)], rs


def _attn_prompt_kernel(bias_ref, q_ref, k_ref, v_ref, lm_ref, o_ref, kbf, vbf, acc, rsum, zbuf, cbuf, tbuf,
                        *, tq, tk, ts):
    h = pl.program_id(1)
    qi = pl.program_id(2)
    nsub = tq // ts
    ndiag = tq // tk

    @pl.when(qi == 0)
    def _():
        kbf[...] = k_ref[0].astype(BF16)
        vbf[...] = v_ref[0].astype(BF16)

    q = (q_ref[0] * (DH_B ** -0.5)).astype(BF16)
    qs = [q[s * ts:(s + 1) * ts] for s in range(nsub)]
    bias = bias_ref[h]
    lm = lm_ref[...]
    row = lax.broadcasted_iota(jnp.int32, (ts, tk), 0)
    col = lax.broadcasted_iota(jnp.int32, (ts, tk), 1)
    started = set()

    q0 = qi * tq

    def diag_subs(d):
        out = []
        for s in range(nsub):
            off = s * ts - d * tk
            if off + ts - 1 > 0:
                out.append((s, None if off >= tk else col < row + off))
        return out

    def diag_qk(d):
        kb = kbf[pl.ds(pl.multiple_of(q0 + d * tk, tk), tk), :]
        return [_dot_nt(qs[s], kb) + bias for s, _ in diag_subs(d)]

    def diag_finish(d, zs):
        cs, tots = _sb_cumulate(zs, [v for _, v in diag_subs(d)], lm, 1)
        return zs, cs, tots

    def diag_apply(d, scores):
        zs, cs, tots = scores
        subs = diag_subs(d)
        vb = vbf[pl.ds(pl.multiple_of(q0 + d * tk, tk), tk), :]
        r0 = [rsum[s * ts:(s + 1) * ts, :] if s in started else jnp.zeros((ts, 1), F32) for s, _ in subs]
        old = [acc[s * ts:(s + 1) * ts, :] if s in started else None for s, _ in subs]
        pvs, rs = _sb_apply(zs, cs, tots, [vb] * len(subs), [v for _, v in subs], r0, False)
        for (s, _), o, pv, r in zip(subs, old, pvs, rs):
            acc[s * ts:(s + 1) * ts, :] = pv if o is None else o + pv
            rsum[s * ts:(s + 1) * ts, :] = r
            started.add(s)

    scores = diag_finish(ndiag - 1, diag_qk(ndiag - 1))
    for d in reversed(range(ndiag)):
        z_next = diag_qk(d - 1) if d > 0 else None
        diag_apply(d, scores)
        if d > 0:
            scores = diag_finish(d - 1, z_next)

    n_past = qi * ndiag

    def past_block(i):
        return pl.multiple_of(jnp.maximum(q0 - (i + 1) * tk, 0), tk)

    rows = [slice(s * ts, (s + 1) * ts) for s in range(nsub)]

    def qk(i):
        kb = kbf[pl.ds(past_block(i), tk), :]
        return [_dot_nt(qs[s], kb) + bias for s in range(nsub)]

    def finish_scores(zs, slot):
        cs, tots = _sb_cumulate(zs, [None] * nsub, lm, 1)
        for r, z, c, t in zip(rows, zs, cs, tots):
            zbuf[slot, r, :] = z
            cbuf[slot, r, :] = c
            tbuf[slot, r, :] = t

    def apply(i, slot):
        vb = vbf[pl.ds(past_block(i), tk), :]
        pvs, rs = _sb_apply([zbuf[slot, r, :] for r in rows], [cbuf[slot, r, :] for r in rows],
                            [tbuf[slot, r, :] for r in rows], [vb] * nsub, [None] * nsub,
                            [rsum[r, :] for r in rows], False)
        for r, pv, rn in zip(rows, pvs, rs):
            acc[r, :] += pv
            rsum[r, :] = rn

    @pl.when(n_past > 0)
    def _():
        finish_scores(qk(0), 0)

    def body(j, carry):
        for slot in range(2):
            i = 2 * j + slot
            z_next = qk(i + 1)
            apply(i, slot)
            finish_scores(z_next, 1 - slot)
        return carry

    lax.fori_loop(0, n_past // 2, body, 0)
    o_ref[0] = acc[...]


def _later_ones(n):
    r = lax.broadcasted_iota(jnp.int32, (n, n), 0)
    c = lax.broadcasted_iota(jnp.int32, (n, n), 1)
    return (r > c).astype(BF16)


def attn_prompt(h, sb_bias):
    b, l, _ = h.shape
    tq = min(1024, l)
    tk = min(256, tq)
    ts = min(512, tq)
    assert l == tq or (tq // tk) % 2 == 0
    qb = QKVB_OFF // DH_B
    kern = functools.partial(_attn_prompt_kernel, tq=tq, tk=tk, ts=ts)
    grid_spec = pltpu.PrefetchScalarGridSpec(
        num_scalar_prefetch=0,
        grid=(b, H_B, l // tq),
        in_specs=[pl.BlockSpec(memory_space=pltpu.SMEM),
                  pl.BlockSpec((1, tq, DH_B), lambda i, hh, j: (i, j, qb + hh)),
                  pl.BlockSpec((1, l, DH_B), lambda i, hh, j: (i, 0, qb + H_B + hh)),
                  pl.BlockSpec((1, l, DH_B), lambda i, hh, j: (i, 0, qb + 2 * H_B + hh)),
                  pl.BlockSpec((tk, tk), lambda i, hh, j: (0, 0))],
        out_specs=pl.BlockSpec((1, tq, DH_B), lambda i, hh, j: (i, j, hh)),
        scratch_shapes=[pltpu.VMEM((l, DH_B), BF16),
                        pltpu.VMEM((l, DH_B), BF16),
                        pltpu.VMEM((tq, DH_B), F32),
                        pltpu.VMEM((tq, 1), F32),
                        pltpu.VMEM((2, tq, tk), F32),
                        pltpu.VMEM((2, tq, tk), F32),
                        pltpu.VMEM((2, tq, 1), F32)])
    return pl.pallas_call(
        kern,
        grid_spec=grid_spec,
        out_shape=jax.ShapeDtypeStruct((b, l, W_B), F32),
        compiler_params=_cparams(("parallel", "parallel", "arbitrary")),
        name="attn_prompt",
    )(sb_bias, h, h, h, _later_ones(tk))


def _attn_sample_kernel(pt_ref, bias_ref, qkv_ref, *rest, seq, pages_per_step, page_group, n_steps):
    g_pages = pages_per_step
    k_refs = rest[:g_pages]
    v_refs = rest[g_pages:2 * g_pages]
    lm_ref = rest[2 * g_pages]
    lx_ref = rest[2 * g_pages + 1]
    o_ref = rest[2 * g_pages + 2]
    qall, acc, rsum = rest[2 * g_pages + 3:]
    s = pl.program_id(1)
    rows = H_B * seq
    pcols = PAGE_SIZE * H_B
    rid = lax.broadcasted_iota(jnp.int32, (rows, 1), 0)
    bias = jnp.zeros((rows, 1), F32)
    for hh in range(H_B):
        bias = jnp.where((rid >= hh * seq) & (rid < (hh + 1) * seq), bias_ref[hh], bias)

    @pl.when(s == 0)
    def _():
        qkv = qkv_ref[0]
        q = qkv[:, 0:W_B] * (DH_B ** -0.5)
        qall[...] = jnp.concatenate([q[:, hh * DH_B:(hh + 1) * DH_B] for hh in range(H_B)], axis=0).astype(BF16)
        lane = lax.broadcasted_iota(jnp.int32, (seq, W_B), 1)
        parts = [jnp.where((lane >= hh * DH_B) & (lane < (hh + 1) * DH_B), q, 0.0) for hh in range(H_B)]
        qbd = jnp.concatenate(parts, axis=0).astype(BF16)
        zrows = jnp.zeros((PAGE_SIZE - seq, W_B), F32)
        k_own = jnp.concatenate([qkv[:, W_B:2 * W_B], zrows], axis=0).astype(BF16)
        v_own = jnp.concatenate([qkv[:, 2 * W_B:3 * W_B], zrows], axis=0).astype(BF16)
        key = lax.broadcasted_iota(jnp.int32, (rows, PAGE_SIZE), 1)
        qpos = lax.broadcasted_iota(jnp.int32, (rows, PAGE_SIZE), 0) % seq
        pvs, rs = _sb_group([(qbd, k_own, v_own, key < qpos)], lm_ref[...], bias, jnp.zeros((rows, 1), F32), True)
        acc[...] = jnp.concatenate(
            [pvs[0][hh * seq:(hh + 1) * seq, hh * DH_B:(hh + 1) * DH_B] for hh in range(H_B)], axis=0)
        rsum[...] = rs[0]

    own = (lax.broadcasted_iota(jnp.int32, (rows, pcols), 1) % H_B
           == lax.broadcasted_iota(jnp.int32, (rows, pcols), 0) // seq)
    qa = qall[...]
    lx = lx_ref[...]
    r = rsum[...]
    total = acc[...]
    for g0 in reversed(range(0, g_pages, page_group)):
        tiles = [(qa, k_refs[gi][0, 0].astype(BF16), v_refs[gi][0, 0].astype(BF16), own)
                 for gi in reversed(range(g0, g0 + page_group))]
        pvs, rs = _sb_group(tiles, lx, bias, r, True, H_B)
        r = rs[-1]
        for pv in pvs:
            total = total + pv
    acc[...] = total
    rsum[...] = r

    @pl.when(s == n_steps - 1)
    def _():
        a = acc[...]
        o_ref[0] = jnp.concatenate([a[hh * seq:(hh + 1) * seq, :] for hh in range(H_B)], axis=1)


def attn_sample(h, cache_k, cache_v, page_table, sb_bias, layer):
    b, seq, _ = h.shape
    depth, n_pool = cache_k.shape[:2]
    n_pages = page_table.shape[0] // b
    g_pages = min(32, n_pages)
    n_steps = n_pages // g_pages
    rows = H_B * seq
    pcols = PAGE_SIZE * H_B
    ck = cache_k.reshape(depth, n_pool, pcols, DH_B)
    cv = cache_v.reshape(depth, n_pool, pcols, DH_B)
    kidx = lax.broadcasted_iota(jnp.int32, (pcols, pcols), 0) // H_B
    lexp = (kidx > kidx.T).astype(BF16)

    def page_map(gi):
        def index_map(i, s, pt):
            return (layer, pt[i * n_pages + (n_steps - 1 - s) * g_pages + gi], 0, 0)
        return index_map

    page_specs = [pl.BlockSpec((1, 1, pcols, DH_B), page_map(gi)) for gi in range(g_pages)]
    kern = functools.partial(_attn_sample_kernel, seq=seq, pages_per_step=g_pages, page_group=min(8, g_pages),
                             n_steps=n_steps)
    grid_spec = pltpu.PrefetchScalarGridSpec(
        num_scalar_prefetch=1,
        grid=(b, n_steps),
        in_specs=([pl.BlockSpec(memory_space=pltpu.SMEM),
                   pl.BlockSpec((1, seq, QKVB_W), lambda i, s, pt: (i, 0, QKVB_OFF // QKVB_W))]
                  + page_specs + page_specs
                  + [pl.BlockSpec((PAGE_SIZE, PAGE_SIZE), lambda i, s, pt: (0, 0)),
                     pl.BlockSpec((pcols, pcols), lambda i, s, pt: (0, 0))]),
        out_specs=pl.BlockSpec((1, seq, W_B), lambda i, s, pt: (i, 0, 0)),
        scratch_shapes=[pltpu.VMEM((rows, DH_B), BF16),
                        pltpu.VMEM((rows, DH_B), F32),
                        pltpu.VMEM((rows, 1), F32)])
    return pl.pallas_call(
        kern,
        grid_spec=grid_spec,
        out_shape=jax.ShapeDtypeStruct((b, seq, W_B), F32),
        compiler_params=_cparams(("parallel", "arbitrary")),
        name="attn_sample",
    )(page_table, sb_bias, h, *([ck] * g_pages), *([cv] * g_pages), _later_ones(PAGE_SIZE), lexp)


def _merge_kernel(x_ref, g_ref, c_ref, oa_ref, ob_ref, wpa_ref, wpb_ref, wpc_ref, wo_ref, conv_ref, ln_ref, sb0_ref,
                  o_ref, tail_ref, ext_scr, *, tm):
    t = pl.program_id(1)

    @pl.when(t == 0)
    def _():
        ext_scr[0:HALO, :] = sb0_ref[0]

    cc = c_ref[0]
    u = cc[:, 2 * W_C:] * cc[:, :W_C]
    ext_scr[HALO:HALO + tm, :] = u
    w = conv_ref[...]
    y = ext_scr[HALO - 2:HALO - 2 + tm, :] * w[0:1, :] + ext_scr[HALO - 1:HALO - 1 + tm, :] * w[1:2, :] + u * w[2:3, :]
    tail = ext_scr[tm:tm + HALO, :]
    ext_scr[0:HALO, :] = tail
    tail_ref[0] = tail
    o_c = cc[:, W_C:2 * W_C] * y

    g = g_ref[0]
    merged = (_sigmoid(g[:, :D_MODEL]) * _dot(oa_ref[0].astype(BF16), wpa_ref[...])
              + _sigmoid(g[:, D_MODEL:2 * D_MODEL]) * _dot(ob_ref[0].astype(BF16), wpb_ref[...])
              + _sigmoid(g[:, 2 * D_MODEL:]) * _dot(o_c.astype(BF16), wpc_ref[...]))
    r = DEEPNORM_ALPHA * x_ref[0] + _dot(merged.astype(BF16), wo_ref[...])
    o_ref[0] = _layer_norm(r, ln_ref[0:1, :], ln_ref[1:2, :])


def merge(x, h, o_a, o_b, wpa, wpb, wpc, wo, conv_c, ln, sb0, layer):
    b, l, _ = x.shape
    tm = min(512, l)
    kern = functools.partial(_merge_kernel, tm=tm)
    const = lambda i, t: (layer, 0, 0)
    return pl.pallas_call(
        kern,
        grid=(b, l // tm),
        in_specs=[pl.BlockSpec((1, tm, D_MODEL), lambda i, t: (i, t, 0)),
                  pl.BlockSpec((1, tm, G_W), lambda i, t: (i, t, G_OFF // G_W)),
                  pl.BlockSpec((1, tm, C_W), lambda i, t: (i, t, C_OFF // C_W)),
                  pl.BlockSpec((1, tm, W_V_A), lambda i, t: (i, t, 0)),
                  pl.BlockSpec((1, tm, W_B), lambda i, t: (i, t, 0)),
                  pl.BlockSpec((None, W_V_A, D_MODEL), const),
                  pl.BlockSpec((None, W_B, D_MODEL), const),
                  pl.BlockSpec((None, W_C, D_MODEL), const),
                  pl.BlockSpec((None, D_MODEL, D_MODEL), const),
                  pl.BlockSpec((None, CONV_C, W_C), const),
                  pl.BlockSpec((None, 2, D_MODEL), const),
                  pl.BlockSpec((1, HALO, W_C), lambda i, t: (i, 0, 0))],
        out_specs=[pl.BlockSpec((1, tm, D_MODEL), lambda i, t: (i, t, 0)),
                   pl.BlockSpec((1, HALO, W_C), lambda i, t: (i, 0, 0))],
        out_shape=[jax.ShapeDtypeStruct((b, l, D_MODEL), F32),
                   jax.ShapeDtypeStruct((b, HALO, W_C), F32)],
        scratch_shapes=[pltpu.VMEM((tm + HALO, W_C), F32)],
        compiler_params=_cparams(("parallel", "arbitrary")),
        name="merge",
    )(x, h, h, o_a, o_b, wpa, wpb, wpc, wo, conv_c, ln, sb0)


def _moe_kernel(x_ref, wr_ref, br_ref, wgu_ref, wdn_ref, ln_ref, o_ref, xb, gates, acc, *, tm, eps):
    e = pl.program_id(1)
    lane = lax.broadcasted_iota(jnp.int32, (tm, LANES), 1)

    @pl.when(e == 0)
    def _():
        x = x_ref[...]
        xh, xl = _split2(x)
        wh, wl = _split2(wr_ref[...])
        logits = _dot(xh, wh) + _dot(xh, wl) + _dot(xl, wh) + br_ref[...]
        lanef = lane.astype(F32)
        big = float(LANES)
        is_g = lane < N_GROUPS
        gl = jnp.where(is_g, logits, -jnp.inf)
        gmax = jnp.max(gl, axis=-1, keepdims=True)
        gsel = jnp.min(jnp.where(gl == gmax, lanef, big), axis=-1, keepdims=True)
        pg_sel = 1.0 / jnp.sum(jnp.where(is_g, jnp.exp(gl - gmax), 0.0), axis=-1, keepdims=True)
        lo = N_GROUPS + gsel * EXPERTS_PER_GROUP
        ev = jnp.where((lanef >= lo) & (lanef < lo + EXPERTS_PER_GROUP), logits, -jnp.inf)
        v1 = jnp.max(ev, axis=-1, keepdims=True)
        i1 = jnp.min(jnp.where(ev == v1, lanef, big), axis=-1, keepdims=True)
        ev2 = jnp.where(lanef == i1, -jnp.inf, ev)
        v2 = jnp.max(ev2, axis=-1, keepdims=True)
        i2 = jnp.min(jnp.where(ev2 == v2, lanef, big), axis=-1, keepdims=True)
        e2 = jnp.exp(v2 - v1)
        den = 1.0 + e2
        gates[...] = jnp.where(lanef == i1, pg_sel / den, jnp.where(lanef == i2, pg_sel * e2 / den, 0.0))
        xb[...] = xh
        acc[...] = jnp.zeros((tm, D_MODEL), F32)

    xv = xb[...]
    g_all = gates[...]
    hhs = [_dot(xv, wgu_ref[j]) for j in range(eps)]
    acts = []
    for j, hh in enumerate(hhs):
        ge = jnp.sum(jnp.where(lane == N_GROUPS + e * eps + j, g_all, 0.0), axis=-1, keepdims=True)
        acts.append((_silu(hh[:, :D_EXPERT]) * hh[:, D_EXPERT:] * ge).astype(BF16))
    acc[...] += _dot(jnp.concatenate(acts, axis=1), wdn_ref[...].reshape(eps * D_EXPERT, D_MODEL))

    @pl.when(e == N_EXPERTS // eps - 1)
    def _():
        r = DEEPNORM_ALPHA * x_ref[...] + acc[...]
        o_ref[...] = _layer_norm(r, ln_ref[0:1, :], ln_ref[1:2, :])


def moe(x2d, wr, br, wgu, wdn, ln, layer):
    t_rows = x2d.shape[0]
    tm = min(1024, t_rows)
    eps = 4
    kern = functools.partial(_moe_kernel, tm=tm, eps=eps)
    return pl.pallas_call(
        kern,
        grid=(t_rows // tm, N_EXPERTS // eps),
        in_specs=[pl.BlockSpec((tm, D_MODEL), lambda i, e: (i, 0)),
                  pl.BlockSpec((None, D_MODEL, LANES), lambda i, e: (layer, 0, 0)),
                  pl.BlockSpec((None, 1, LANES), lambda i, e: (layer, 0, 0)),
                  pl.BlockSpec((eps, D_MODEL, 2 * D_EXPERT), lambda i, e: (layer * (N_EXPERTS // eps) + e, 0, 0)),
                  pl.BlockSpec((eps, D_EXPERT, D_MODEL), lambda i, e: (layer * (N_EXPERTS // eps) + e, 0, 0)),
                  pl.BlockSpec((None, 2, D_MODEL), lambda i, e: (layer, 0, 0))],
        out_specs=pl.BlockSpec((tm, D_MODEL), lambda i, e: (i, 0)),
        out_shape=jax.ShapeDtypeStruct((t_rows, D_MODEL), F32),
        scratch_shapes=[pltpu.VMEM((tm, D_MODEL), BF16),
                        pltpu.VMEM((tm, LANES), F32),
                        pltpu.VMEM((tm, D_MODEL), F32)],
        compiler_params=_cparams(("parallel", "arbitrary")),
        name="moe",
    )(x2d, wr, br, wgu, wdn, ln)


def _prep_params(w_in, conv_a, a_log, dt_bias, norm_a, conv_c, w_pa, w_pb, w_pc, w_o, ln_g, ln_b,
                 w_rg, b_rg, w_re, b_re, w_gu, w_down):
    depth = w_in.shape[0]
    o_z = W_QKV_A
    o_ba = o_z + W_V_A
    o_qb = o_ba + 2 * H_A
    o_c = o_qb + 3 * W_B
    o_g = o_c + 3 * W_C
    w_main = jnp.concatenate([w_in[:, :, o_g:], w_in[:, :, :W_QKV_A], w_in[:, :, o_c:o_g], w_in[:, :, o_qb:o_c],
                              w_in[:, :, o_z:o_ba]], axis=2).astype(BF16)
    w_ba = jnp.pad(w_in[:, :, o_ba:o_qb], ((0, 0), (0, 0), (0, LANES - 2 * H_A))).astype(BF16)
    prm = jnp.zeros((depth, SUBLANES, LANES), F32)
    prm = prm.at[:, 0, H_A:2 * H_A].set(a_log).at[:, 1, H_A:2 * H_A].set(dt_bias).at[:, 2, :DV_A].set(norm_a)
    pad_r = LANES - N_GROUPS - N_EXPERTS
    w_r = jnp.pad(jnp.concatenate([w_rg, w_re], axis=2), ((0, 0), (0, 0), (0, pad_r)))
    b_r = jnp.pad(jnp.concatenate([b_rg, b_re], axis=1), ((0, 0), (0, pad_r))).reshape(depth, 1, LANES)
    return dict(
        w_main=w_main, w_ba=w_ba, conv_a=conv_a, prm=prm, conv_c=conv_c,
        w_pa=w_pa.astype(BF16), w_pb=w_pb.astype(BF16), w_pc=w_pc.astype(BF16), w_o=w_o.astype(BF16),
        ln0=jnp.stack([ln_g[:, 0], ln_b[:, 0]], axis=1), ln1=jnp.stack([ln_g[:, 1], ln_b[:, 1]], axis=1),
        w_r=w_r, b_r=b_r,
        w_gu=w_gu.astype(BF16).reshape(depth * N_EXPERTS, D_MODEL, 2 * D_EXPERT),
        w_down=w_down.astype(BF16).reshape(depth * N_EXPERTS, D_EXPERT, D_MODEL))


def _pad_tail(buf):
    return jnp.pad(buf, ((0, 0), (HALO - buf.shape[1], 0), (0, 0)))


def _layer(x, p, s0, db0, sb0, attn_fn, layer, depth, k_buf, v_buf):
    b, l, _ = x.shape
    h, k_buf, v_buf = in_proj(x.reshape(b * l, D_MODEL), p["w_main"], layer, depth, k_buf, v_buf)
    h = h.reshape(b, l, H_COLS)
    o_a, s_new, dtail = delta_mixer(x, h, p["w_ba"], p["conv_a"], p["prm"], s0, _pad_tail(db0), layer)
    o_b = attn_fn(h)
    x1, stail = merge(x, h, o_a, o_b, p["w_pa"], p["w_pb"], p["w_pc"], p["w_o"], p["conv_c"], p["ln0"],
                      _pad_tail(sb0), layer)
    x2 = moe(x1.reshape(b * l, D_MODEL), p["w_r"], p["b_r"], p["w_gu"], p["w_down"], p["ln1"], layer)
    x2 = x2.reshape(b, l, D_MODEL)
    return (x2, s_new, dtail[:, HALO - (CONV_A - 1):], stail[:, HALO - (CONV_C - 1):]), k_buf, v_buf


def kernel(x_prompt, x_sample, cache_k, cache_v, state_delta, state_dconv, state_sconv, page_table, w_in, conv_a, a_log, dt_bias, norm_a, sb_bias, conv_c, w_pa, w_pb, w_pc, w_o, ln_g, ln_b, w_rg, b_rg, w_re, b_re, w_gu, w_down):
    bp, lp, _ = x_prompt.shape
    bs, ls, _ = x_sample.shape
    depth = w_in.shape[0]
    pt = page_table.reshape(-1).astype(jnp.int32)
    xp, xs = x_prompt, x_sample
    outs_p = [[] for _ in range(3)]
    outs_s = [[] for _ in range(3)]
    pk = pv = sk = sv = None
    p = _prep_params(w_in, conv_a, a_log, dt_bias, norm_a, conv_c, w_pa, w_pb, w_pc, w_o, ln_g, ln_b,
                     w_rg, b_rg, w_re, b_re, w_gu, w_down)
    for l in range(depth):
        bias = sb_bias[l]
        res, pk, pv = _layer(xp, p,
                             jnp.zeros((bp, H_A, DK_A, DV_A), F32),
                             jnp.zeros((bp, CONV_A - 1, W_QKV_A), F32),
                             jnp.zeros((bp, CONV_C - 1, W_C), F32),
                             lambda h: attn_prompt(h, bias), l, depth, pk, pv)
        xp = res[0]
        for acc_list, r in zip(outs_p, res[1:]):
            acc_list.append(r)
        res, sk, sv = _layer(xs, p, state_delta[l], state_dconv[l], state_sconv[l],
                             lambda h: attn_sample(h, cache_k, cache_v, pt, bias, l), l, depth, sk, sv)
        xs = res[0]
        for acc_list, r in zip(outs_s, res[1:]):
            acc_list.append(r)
    return (xp, xs, *[jnp.stack(o) for o in outs_p],
            pk.reshape(depth, bp, lp, H_B, DH_B), pv.reshape(depth, bp, lp, H_B, DH_B),
            *[jnp.stack(o) for o in outs_s],
            sk.reshape(depth, bs, ls, H_B, DH_B), sv.reshape(depth, bs, ls, H_B, DH_B))
```
